```python
import math
import jax, jax.numpy as jnp
from jax import lax
import numpy as np

D_MODEL = 1024
BATCH = 4
SEQ = 4096
DEPTH = 2
DEC_BATCH = 32
DEC_SEQ = 8
PAST_LEN = 16384
PAGE_SIZE = 128

A_HEADS = 8
A_HEAD_DIM = 64
A_WIDTH = A_HEADS * A_HEAD_DIM
DILATED_PATTERNS = ((128, 1), (512, 4), (2048, 16))
WIN_MAX = max(w for w, _ in DILATED_PATTERNS)
B_HEADS = 4
B_KEY_DIM = 128
B_VAL_DIM = 128
B_KEY_WIDTH = B_HEADS * B_KEY_DIM
B_VAL_WIDTH = B_HEADS * B_VAL_DIM
HGRN_CHUNK = 64
D_FF = 2816
NORM_EPS = 1e-6
IN_SPLIT_SIZES = (A_WIDTH, A_WIDTH, A_WIDTH, B_KEY_WIDTH, B_KEY_WIDTH, B_VAL_WIDTH, B_VAL_WIDTH)
IN_WIDTH = sum(IN_SPLIT_SIZES)
MIX_WIDTH = A_WIDTH + B_VAL_WIDTH

kernel_name = 'hybrid_dilated_attn_hgrn2_macaron_step'


def rmsnorm(x, g):
    xf = x.astype(jnp.float32)
    y = xf * lax.rsqrt(jnp.mean(xf * xf, axis=-1, keepdims=True) + NORM_EPS)
    return (y * g.astype(jnp.float32)).astype(x.dtype)


def swiglu(x, w_gate, w_up, w_down):
    return (jax.nn.silu(x @ w_gate) * (x @ w_up)) @ w_down


def dilated_window_prompt(q, k, v, window, dilation):
    B, S, H, Dh = q.shape
    steps = window // dilation
    n = S // dilation
    nb = -(-n // steps)
    pad = nb * steps - n

    def to_blocks(t):
        t = t.reshape(B, n, dilation, H, Dh).transpose(0, 2, 1, 3, 4)
        t = jnp.pad(t, ((0, 0), (0, 0), (0, pad), (0, 0), (0, 0)))
        return t.reshape(B, dilation, nb, steps, H, Dh)

    def with_prev(t):
        prev = jnp.pad(t, ((0, 0), (0, 0), (1, 0), (0, 0), (0, 0), (0, 0)))[:, :, :-1]
        return jnp.concatenate([prev, t], axis=3)

    qb = to_blocks(q)
    kw = with_prev(to_blocks(k))
    vw = with_prev(to_blocks(v))
    s = jnp.einsum('brcqhd,brckhd->brchqk', qb, kw).astype(jnp.float32) / math.sqrt(Dh)
    qi = jnp.arange(steps)[:, None]
    ki = jnp.arange(2 * steps)[None, :]
    dist = steps + qi - ki
    band = (dist >= 0) & (dist <= steps)
    exists = (jnp.arange(nb) > 0)[:, None, None] | (ki >= steps)[None]
    mask = band[None] & exists
    s = jnp.where(mask[:, None], s, -jnp.inf)
    lse = jax.nn.logsumexp(s, axis=-1)
    p = jnp.exp(s - lse[..., None])
    o = jnp.einsum('brchqk,brckhd->brcqhd', p, vw.astype(jnp.float32))
    o = o.reshape(B, dilation, nb * steps, H, Dh)[:, :, :n]
    o = o.transpose(0, 2, 1, 3, 4).reshape(B, S, H, Dh)
    lse = lse.transpose(0, 1, 2, 4, 3).reshape(B, dilation, nb * steps, H)[:, :, :n]
    lse = lse.transpose(0, 2, 1, 3).reshape(B, S, H)
    return o, lse


def dilated_window_sample(q, k_all, v_all, window, dilation):
    B, T, H, Dh = q.shape
    N = k_all.shape[1]
    steps = window // dilation
    idx = (N - T + jnp.arange(T))[:, None] - dilation * jnp.arange(steps + 1)[None, :]
    valid = idx >= 0
    idx = jnp.maximum(idx, 0)
    kg = k_all[:, idx]
    vg = v_all[:, idx]
    s = jnp.einsum('bthd,btjhd->bthj', q, kg).astype(jnp.float32) / math.sqrt(Dh)
    s = jnp.where(valid[:, None, :], s, -jnp.inf)
    lse = jax.nn.logsumexp(s, axis=-1)
    p = jnp.exp(s - lse[..., None])
    o = jnp.einsum('bthj,btjhd->bthd', p, vg.astype(jnp.float32))
    return o, lse


def dilated_mixture(outs):
    o = jnp.stack([oi for oi, _ in outs])
    lse = jnp.stack([li for _, li in outs])
    w = jax.nn.softmax(lse, axis=0)
    return jnp.sum(w[..., None] * o, axis=0)


def hgrn2_scan(q, k, v, log_f, s0, chunk):
    B, T, H, K = q.shape
    V = v.shape[-1]
    nc = T // chunk

    def split(t):
        return t.reshape(B, nc, chunk, H, t.shape[-1]).transpose(1, 0, 3, 2, 4)

    causal = jnp.tril(jnp.ones((chunk, chunk), dtype=bool))

    def step(S, xs):
        qc, kc, vc, gc = xs
        G = jnp.cumsum(gc, axis=2)
        o_inter = jnp.einsum('bhck,bhkv->bhcv', qc * jnp.exp(G), S)
        diff = G[:, :, :, None, :] - G[:, :, None, :, :]
        decay = jnp.exp(jnp.where(causal[:, :, None], diff, -jnp.inf))
        A = jnp.einsum('bhtk,bhsk,bhtsk->bhts', qc, kc, decay)
        o = o_inter + jnp.einsum('bhts,bhsv->bhtv', A, vc)
        G_last = G[:, :, -1:, :]
        S_new = jnp.exp(G_last[:, :, 0, :])[..., None] * S + jnp.einsum(
            'bhck,bhcv->bhkv', kc * jnp.exp(G_last - G), vc)
        return S_new, o

    S_fin, o = lax.scan(step, s0, (split(q), split(k), split(v), split(log_f)))
    o = o.transpose(1, 0, 3, 2, 4).reshape(B, T, H, V)
    return o, S_fin


def token_mixing(hn, w_in_l, attn_g_l, lb_l, hgrn_g_l, w_out_l, k_past, v_past, s0):
    B, T, _ = hn.shape
    f32 = jnp.float32
    z = hn @ w_in_l
    split_at = [int(c) for c in np.cumsum(IN_SPLIT_SIZES)[:-1]]
    aq, ak, av, bq, bf, bi, bg = jnp.split(z, split_at, axis=-1)
    aq = aq.reshape(B, T, A_HEADS, A_HEAD_DIM)
    ak = ak.reshape(B, T, A_HEADS, A_HEAD_DIM)
    av = av.reshape(B, T, A_HEADS, A_HEAD_DIM)

    if k_past is None:
        outs = [dilated_window_prompt(aq, ak, av, w, d) for (w, d) in DILATED_PATTERNS]
        keep = min(WIN_MAX, T)
        k_new, v_new = ak[:, T - keep:], av[:, T - keep:]
        s_init = jnp.zeros((B, B_HEADS, B_KEY_DIM, B_VAL_DIM), f32)
    else:
        k_all = jnp.concatenate([k_past.astype(ak.dtype), ak], axis=1)
        v_all = jnp.concatenate([v_past.astype(av.dtype), av], axis=1)
        outs = [dilated_window_sample(aq, k_all, v_all, w, d) for (w, d) in DILATED_PATTERNS]
        k_new, v_new = ak, av
        s_init = s0.astype(f32)
    o_a = rmsnorm(dilated_mixture(outs).reshape(B, T, A_WIDTH), attn_g_l)

    qh = jax.nn.silu(bq.astype(f32)).reshape(B, T, B_HEADS, B_KEY_DIM)
    f = lb_l + (1.0 - lb_l) * jax.nn.sigmoid(bf.astype(f32))
    f = f.reshape(B, T, B_HEADS, B_KEY_DIM)
    log_f = jnp.log(f)
    kh = 1.0 - f
    vh = bi.astype(f32).reshape(B, T, B_HEADS, B_VAL_DIM)
    chunk = HGRN_CHUNK if T % HGRN_CHUNK == 0 else T
    o_b, s_fin = hgrn2_scan(qh, kh, vh, log_f, s_init, chunk)
    o_b = rmsnorm(o_b, hgrn_g_l) * jax.nn.silu(bg.astype(f32).reshape(B, T, B_HEADS, B_VAL_DIM))

    o = jnp.concatenate([o_a, o_b.reshape(B, T, B_VAL_WIDTH)], axis=-1).astype(hn.dtype)
    return o @ w_out_l, k_new, v_new, s_fin


def setup_inputs(seed: int = 0) -> dict:
    key = jax.random.key(seed)
    ks = iter(jax.random.split(key, 32))
    f32 = jnp.float32
    win_buf = min(WIN_MAX, PAST_LEN)

    def w(shape, fan_in):
        return jax.random.normal(next(ks), shape, f32) * fan_in ** -0.5

    def gain(shape):
        return 1.0 + 0.02 * jax.random.normal(next(ks), shape, f32)

    return {
        'x_prompt': jax.random.normal(next(ks), (BATCH, SEQ, D_MODEL), f32),
        'x_sample': jax.random.normal(next(ks), (DEC_BATCH, DEC_SEQ, D_MODEL), f32),
        'cache_attn_k': jax.random.normal(next(ks), (DEPTH, DEC_BATCH, win_buf, A_HEADS, A_HEAD_DIM), f32),
        'cache_attn_v': jax.random.normal(next(ks), (DEPTH, DEC_BATCH, win_buf, A_HEADS, A_HEAD_DIM), f32),
        'state_hgrn': 0.3 * jax.random.normal(next(ks), (DEPTH, DEC_BATCH, B_HEADS, B_KEY_DIM, B_VAL_DIM), f32),
        'ff1_pre_g': gain((DEPTH, D_MODEL)),
        'ff1_w_gate': w((DEPTH, D_MODEL, D_FF), D_MODEL),
        'ff1_w_up': w((DEPTH, D_MODEL, D_FF), D_MODEL),
        'ff1_w_down': w((DEPTH, D_FF, D_MODEL), D_FF),
        'ff1_post_g': gain((DEPTH, D_MODEL)),
        'mix_pre_g': gain((DEPTH, D_MODEL)),
        'w_in': w((DEPTH, D_MODEL, IN_WIDTH), D_MODEL),
        'attn_norm_g': gain((DEPTH, A_WIDTH)),
        'hgrn_lb_logits': 0.5 * jax.random.normal(next(ks), (DEPTH, B_KEY_WIDTH), f32),
        'hgrn_norm_g': gain((DEPTH, B_VAL_DIM)),
        'w_out': w((DEPTH, MIX_WIDTH, D_MODEL), MIX_WIDTH),
        'mix_post_g': gain((DEPTH, D_MODEL)),
        'ff2_pre_g': gain((DEPTH, D_MODEL)),
        'ff2_w_gate': w((DEPTH, D_MODEL, D_FF), D_MODEL),
        'ff2_w_up': w((DEPTH, D_MODEL, D_FF), D_MODEL),
        'ff2_w_down': w((DEPTH, D_FF, D_MODEL), D_FF),
        'ff2_post_g': gain((DEPTH, D_MODEL)),
    }


def reference(x_prompt, x_sample, cache_attn_k, cache_attn_v, state_hgrn,
              ff1_pre_g, ff1_w_gate, ff1_w_up, ff1_w_down, ff1_post_g,
              mix_pre_g, w_in, attn_norm_g, hgrn_lb_logits, hgrn_norm_g, w_out, mix_post_g,
              ff2_pre_g, ff2_w_gate, ff2_w_up, ff2_w_down, ff2_post_g):
    lb_soft = jax.nn.softmax(hgrn_lb_logits.astype(jnp.float32), axis=0)
    lower_bounds = jnp.cumsum(lb_soft, axis=0) - lb_soft[0]

    def layer(x, l, k_past, v_past, s0):
        h = x + 0.5 * rmsnorm(swiglu(rmsnorm(x, ff1_pre_g[l]), ff1_w_gate[l], ff1_w_up[l], ff1_w_down[l]),
                              ff1_post_g[l])
        m, k_new, v_new, s_new = token_mixing(rmsnorm(h, mix_pre_g[l]), w_in[l], attn_norm_g[l],
                                              lower_bounds[l], hgrn_norm_g[l], w_out[l], k_past, v_past, s0)
        h = h + rmsnorm(m, mix_post_g[l])
        h = h + 0.5 * rmsnorm(swiglu(rmsnorm(h, ff2_pre_g[l]), ff2_w_gate[l], ff2_w_up[l], ff2_w_down[l]),
                              ff2_post_g[l])
        return h, k_new, v_new, s_new

    hp, hs = x_prompt, x_sample
    kp_l, vp_l, sp_l, ks_l, vs_l, ss_l = [], [], [], [], [], []
    for l in range(DEPTH):
        hp, kp, vp, sp = layer(hp, l, None, None, None)
        hs, kd, vd, sd = layer(hs, l, cache_attn_k[l], cache_attn_v[l], state_hgrn[l])
        kp_l.append(kp); vp_l.append(vp); sp_l.append(sp)
        ks_l.append(kd); vs_l.append(vd); ss_l.append(sd)
    new_k_prompt = jnp.stack(kp_l)
    new_v_prompt = jnp.stack(vp_l)
    new_state_prompt = jnp.stack(sp_l)
    new_k_sample = jnp.stack(ks_l)
    new_v_sample = jnp.stack(vs_l)
    new_state_sample = jnp.stack(ss_l)
    return (hp, hs, new_k_prompt, new_v_prompt, new_state_prompt, new_k_sample, new_v_sample, new_state_sample)
```

```python
import functools
import math

import jax
import jax.numpy as jnp
from jax import lax
from jax.experimental import pallas as pl
from jax.experimental.pallas import tpu as pltpu

F32 = jnp.float32
BF16 = jnp.bfloat16

NORM_EPS = 1e-6
A_HEADS = 8
A_HEAD_DIM = 64
A_WIDTH = A_HEADS * A_HEAD_DIM
B_HEADS = 4
B_DIM = 128
B_WIDTH = B_HEADS * B_DIM
DILATIONS = (1, 4, 16)
WINDOW_STEPS = 128
N_SPLITS = 7
HGRN_CHUNK = 64
HGRN_SUB = 16

V7X_VMEM_LIMIT_BYTES = 56 * 1024 * 1024


def _cparams(n_grid_axes):
    sem = ("parallel",) + ("arbitrary",) * (n_grid_axes - 1)
    return pltpu.CompilerParams(dimension_semantics=sem,
                                vmem_limit_bytes=V7X_VMEM_LIMIT_BYTES)


def _rms(x, g):
    return x * lax.rsqrt(jnp.mean(x * x, axis=-1, keepdims=True) + NORM_EPS) * g


def _sigmoid(x):
    return 1.0 / (1.0 + jnp.exp(-x))


def _dot(a, b):
    return jnp.dot(a, b, preferred_element_type=F32)


def _dot_nt(a, b):
    return lax.dot_general(a, b, (((1,), (1,)), ((), ())), preferred_element_type=F32)


def _dot_tn(a, b):
    return lax.dot_general(a, b, (((0,), (0,)), ((), ())), preferred_element_type=F32)


def _ffn_kernel(x_ref, pre_g_ref, wg_ref, wu_ref, wd_ref, post_g_ref, o_ref, xn_ref, acc_ref):
    j = pl.program_id(1)

    @pl.when(j == 0)
    def _():
        xn_ref[...] = _rms(x_ref[...], pre_g_ref[...]).astype(BF16)
        acc_ref[...] = jnp.zeros_like(acc_ref)

    xn = xn_ref[...]
    g = _dot(xn, wg_ref[...])
    u = _dot(xn, wu_ref[...])
    a = (g * _sigmoid(g) * u).astype(BF16)
    acc_ref[...] += _dot(a, wd_ref[...])

    @pl.when(j == pl.num_programs(1) - 1)
    def _():
        o_ref[...] = x_ref[...] + 0.5 * _rms(acc_ref[...], post_g_ref[...])


def _ffn(x, pre_g, wg, wu, wd, post_g, layer, tm, tf):
    m, d = x.shape
    ff = wg.shape[-1]
    return pl.pallas_call(
        _ffn_kernel,
        grid=(m // tm, ff // tf),
        in_specs=[
            pl.BlockSpec((tm, d), lambda i, j: (i, 0)),
            pl.BlockSpec((None, 1, d), lambda i, j: (layer, 0, 0)),
            pl.BlockSpec((None, d, tf), lambda i, j: (layer, 0, j)),
            pl.BlockSpec((None, d, tf), lambda i, j: (layer, 0, j)),
            pl.BlockSpec((None, tf, d), lambda i, j: (layer, j, 0)),
            pl.BlockSpec((None, 1, d), lambda i, j: (layer, 0, 0)),
        ],
        out_specs=pl.BlockSpec((tm, d), lambda i, j: (i, 0)),
        out_shape=jax.ShapeDtypeStruct((m, d), F32),
        scratch_shapes=[pltpu.VMEM((tm, d), BF16), pltpu.VMEM((tm, d), F32)],
        compiler_params=_cparams(2),
        name="ffn",
    )(x, pre_g, wg, wu, wd, post_g)


def _mix_in_kernel(h_ref, g_ref, w_ref, z_ref, hn_ref):
    @pl.when(pl.program_id(1) == 0)
    def _():
        hn_ref[...] = _rms(h_ref[...], g_ref[...]).astype(BF16)

    z_ref[...] = _dot(hn_ref[...], w_ref[...])


def _mix_in(h, g, w_in, layer, tm):
    m, d = h.shape
    n = w_in.shape[-1]
    tn = A_WIDTH
    return pl.pallas_call(
        _mix_in_kernel,
        grid=(m // tm, n // tn),
        in_specs=[
            pl.BlockSpec((tm, d), lambda i, j: (i, 0)),
            pl.BlockSpec((None, 1, d), lambda i, j: (layer, 0, 0)),
            pl.BlockSpec((None, d, tn), lambda i, j: (layer, 0, j)),
        ],
        out_specs=pl.BlockSpec((tm, tn), lambda i, j: (i, j)),
        out_shape=jax.ShapeDtypeStruct((m, n), F32),
        scratch_shapes=[pltpu.VMEM((tm, d), BF16)],
        compiler_params=_cparams(2),
        name="mix_in",
    )(h, g, w_in)


def _attn_prompt_kernel(q_ref, kp_ref, kc_ref, vp_ref, vc_ref, o_ref, lse_ref):
    c = pl.program_id(2)
    w = WINDOW_STEPS
    qi = lax.broadcasted_iota(jnp.int32, (w, 2 * w), 0)
    ki = lax.broadcasted_iota(jnp.int32, (w, 2 * w), 1)
    dist = w + qi - ki
    mask = (dist >= 0) & (dist <= w) & ((ki >= w) | (c > 0))
    scale = 1.0 / math.sqrt(A_HEAD_DIM)
    for h in range(A_HEADS):
        sl = slice(h * A_HEAD_DIM, (h + 1) * A_HEAD_DIM)
        qh = (q_ref[:, sl] * scale).astype(BF16)
        kh = jnp.concatenate([kp_ref[:, sl], kc_ref[:, sl]], axis=0).astype(BF16)
        vh = jnp.concatenate([vp_ref[:, sl], vc_ref[:, sl]], axis=0).astype(BF16)
        s = jnp.where(mask, _dot_nt(qh, kh), -jnp.inf)
        m = jnp.max(s, axis=-1, keepdims=True)
        p = jnp.exp(s - m)
        l = jnp.sum(p, axis=-1, keepdims=True)
        acc = _dot(p.astype(BF16), vh)
        o_ref[:, sl] = acc / l
        lse_ref[:, sl] = jnp.broadcast_to(m + jnp.log(l), (w, A_HEAD_DIM))


def _attn_prompt(z, batch, seq, dil):
    w = WINDOW_STEPS
    n = seq // dil
    nb = n // w
    zv = z.reshape(batch, n, dil * N_SPLITS * A_WIDTH)

    def spec(col, prev):
        if prev:
            return pl.BlockSpec((None, w, A_WIDTH),
                                lambda b, r, c: (b, jnp.maximum(c - 1, 0), r * N_SPLITS + col))
        return pl.BlockSpec((None, w, A_WIDTH), lambda b, r, c: (b, c, r * N_SPLITS + col))

    out_spec = pl.BlockSpec((None, w, A_WIDTH), lambda b, r, c: (b, c, r))
    shape = jax.ShapeDtypeStruct((batch, n, dil * A_WIDTH), F32)
    o, lse = pl.pallas_call(
        _attn_prompt_kernel,
        grid=(batch, dil, nb),
        in_specs=[spec(0, False), spec(1, True), spec(1, False), spec(2, True), spec(2, False)],
        out_specs=[out_spec, out_spec],
        out_shape=[shape, shape],
        compiler_params=_cparams(3),
        name=f"attn_prompt_d{dil}",
    )(zv, zv, zv, zv, zv)
    return o.reshape(batch * seq, A_WIDTH), lse.reshape(batch * seq, A_WIDTH)


def _pattern_count(delta):
    cnt = jnp.zeros(delta.shape, F32)
    for dil in DILATIONS:
        hit = (delta >= 0) & (delta % dil == 0) & (delta <= dil * WINDOW_STEPS)
        cnt = cnt + hit.astype(F32)
    return cnt


def _attn_sample_kernel(q_ref, kn_ref, vn_ref, kc_ref, vc_ref, o_ref):
    t = q_ref.shape[0]
    n_cache = kc_ref.shape[0]
    qi_c = lax.broadcasted_iota(jnp.int32, (t, n_cache), 0)
    ki_c = lax.broadcasted_iota(jnp.int32, (t, n_cache), 1)
    cnt_c = _pattern_count(n_cache + qi_c - ki_c)
    qi_n = lax.broadcasted_iota(jnp.int32, (t, t), 0)
    ki_n = lax.broadcasted_iota(jnp.int32, (t, t), 1)
    cnt_n = _pattern_count(qi_n - ki_n)
    scale = 1.0 / math.sqrt(A_HEAD_DIM)
    for h in range(A_HEADS):
        sl = slice(h * A_HEAD_DIM, (h + 1) * A_HEAD_DIM)
        qh = (q_ref[:, sl] * scale).astype(BF16)
        s_c = jnp.where(cnt_c > 0, _dot_nt(qh, kc_ref[:, sl].astype(BF16)), -jnp.inf)
        s_n = jnp.where(cnt_n > 0, _dot_nt(qh, kn_ref[:, sl].astype(BF16)), -jnp.inf)
        m = jnp.maximum(jnp.max(s_c, axis=-1, keepdims=True), jnp.max(s_n, axis=-1, keepdims=True))
        p_c = cnt_c * jnp.exp(s_c - m)
        p_n = cnt_n * jnp.exp(s_n - m)
        l = jnp.sum(p_c, axis=-1, keepdims=True) + jnp.sum(p_n, axis=-1, keepdims=True)
        acc = _dot(p_c.astype(BF16), vc_ref[:, sl].astype(BF16)) + _dot(p_n.astype(BF16), vn_ref[:, sl].astype(BF16))
        o_ref[:, sl] = acc / l


def _attn_sample(z, cache_k, cache_v, layer, batch, t):
    n_cache = cache_k.shape[2]
    zv = z.reshape(batch, t, N_SPLITS * A_WIDTH)
    zspec = lambda col: pl.BlockSpec((None, t, A_WIDTH), lambda b: (b, 0, col))
    cspec = pl.BlockSpec((None, None, n_cache, A_WIDTH), lambda b: (layer, b, 0, 0))
    o = pl.pallas_call(
        _attn_sample_kernel,
        grid=(batch,),
        in_specs=[zspec(0), zspec(1), zspec(2), cspec, cspec],
        out_specs=pl.BlockSpec((None, t, A_WIDTH), lambda b: (b, 0, 0)),
        out_shape=jax.ShapeDtypeStruct((batch, t, A_WIDTH), F32),
        compiler_params=_cparams(1),
        name="attn_sample",
    )(zv, zv, zv, cache_k, cache_v)
    return o.reshape(batch * t, A_WIDTH)


def _hgrn_kernel(xq_ref, xf_ref, xi_ref, xg_ref, lbl_ref, g_ref, s0_ref, o_ref, sfin_ref, st_ref,
                 *, layer, chunk, sub):
    t_id = pl.program_id(1)

    @pl.when(t_id == 0)
    def _():
        for h in range(B_HEADS):
            st_ref[h] = s0_ref[h].T

    logits = lbl_ref[...]
    e = jnp.exp(logits - jnp.max(logits, axis=0, keepdims=True))
    soft = e / jnp.sum(e, axis=0, keepdims=True)
    lb_all = soft[0:1] - soft[0:1]
    for l in range(1, layer + 1):
        lb_all = lb_all + soft[l:l + 1]

    n_chunks = xq_ref.shape[0] // chunk
    n_sub = chunk // sub
    tri = (lax.broadcasted_iota(jnp.int32, (chunk, chunk), 0)
           >= lax.broadcasted_iota(jnp.int32, (chunk, chunk), 1)).astype(BF16)
    row_c = lax.broadcasted_iota(jnp.int32, (chunk, B_DIM), 0)
    row_s = lax.broadcasted_iota(jnp.int32, (sub, B_DIM), 0)
    lane_c = lax.broadcasted_iota(jnp.int32, (sub, chunk), 1)

    for ci in range(n_chunks):
        rows = slice(ci * chunk, (ci + 1) * chunk)
        for h in range(B_HEADS):
            cols = slice(h * B_DIM, (h + 1) * B_DIM)
            xq = xq_ref[rows, cols]
            xg = xg_ref[rows, cols]
            v = xi_ref[rows, cols]
            lb = lb_all[:, cols]
            qh = xq * _sigmoid(xq)
            f = lb + (1.0 - lb) * _sigmoid(xf_ref[rows, cols])
            logf = jnp.log(f)
            kh = 1.0 - f

            hi = logf.astype(BF16)
            r1 = logf - hi.astype(F32)
            mid = r1.astype(BF16)
            lo = (r1 - mid.astype(F32)).astype(BF16)
            gc = _dot(tri, hi) + _dot(tri, mid) + _dot(tri, lo)

            st = st_ref[h]
            o = _dot_nt((qh * jnp.exp(gc)).astype(BF16), st.astype(BF16))

            a_rows = []
            for i in range(n_sub):
                r0 = i * sub
                gi = gc[r0:r0 + sub]
                qi = qh[r0:r0 + sub]
                ki = kh[r0:r0 + sub]
                if i > 0:
                    bnd = gc[r0 - 1:r0]
                    qo = (qi * jnp.exp(gi - bnd)).astype(BF16)
                    kk = (kh * jnp.exp(jnp.where(row_c < r0, bnd - gc, -jnp.inf))).astype(BF16)
                    a_i = _dot_nt(qo, kk)
                else:
                    a_i = jnp.zeros((sub, chunk), F32)
                for s in range(sub):
                    dec = jnp.exp(jnp.where(row_s >= s, gi - gi[s:s + 1], -jnp.inf))
                    col = jnp.sum(qi * ki[s:s + 1] * dec, axis=-1, keepdims=True)
                    a_i = jnp.where(lane_c == r0 + s, col, a_i)
                a_rows.append(a_i)
            a = a_rows[0] if n_sub == 1 else jnp.concatenate(a_rows, axis=0)
            v16 = v.astype(BF16)
            o = o + _dot(a.astype(BF16), v16)

            g_last = gc[chunk - 1:chunk]
            kd = (kh * jnp.exp(g_last - gc)).astype(BF16)
            st_ref[h] = jnp.exp(g_last) * st + _dot_tn(v16, kd)

            o_ref[rows, cols] = _rms(o, g_ref[...]) * (xg * _sigmoid(xg))

    @pl.when(t_id == pl.num_programs(1) - 1)
    def _():
        for h in range(B_HEADS):
            sfin_ref[h] = st_ref[h].T


def _hgrn(z, lb_logits, norm_g, s0, layer, batch, seq, tt, chunk, sub):
    zv = z.reshape(batch, seq, N_SPLITS * A_WIDTH)
    zspec = lambda col: pl.BlockSpec((None, tt, B_WIDTH), lambda b, t: (b, t, col))
    sspec = pl.BlockSpec((None, B_HEADS, B_DIM, B_DIM), lambda b, t: (b, 0, 0, 0))
    depth = lb_logits.shape[0]
    o, sfin = pl.pallas_call(
        functools.partial(_hgrn_kernel, layer=layer, chunk=chunk, sub=sub),
        grid=(batch, seq // tt),
        in_specs=[zspec(3), zspec(4), zspec(5), zspec(6),
                  pl.BlockSpec((depth, B_WIDTH), lambda b, t: (0, 0)),
                  pl.BlockSpec((None, 1, B_DIM), lambda b, t: (layer, 0, 0)),
                  sspec],
        out_specs=[pl.BlockSpec((None, tt, B_WIDTH), lambda b, t: (b, t, 0)), sspec],
        out_shape=[jax.ShapeDtypeStruct((batch, seq, B_WIDTH), F32),
                   jax.ShapeDtypeStruct((batch, B_HEADS, B_DIM, B_DIM), F32)],
        scratch_shapes=[pltpu.VMEM((B_HEADS, B_DIM, B_DIM), F32)],
        compiler_params=_cparams(2),
        name="hgrn",
    )(zv, zv, zv, zv, lb_logits, norm_g, s0)
    return o.reshape(batch * seq, B_WIDTH), sfin


def _mix_out_kernel(*refs, n_patterns):
    att_refs = refs[:2 * n_patterns] if n_patterns > 1 else refs[:1]
    ob_ref, h_ref, attn_g_ref, w_ref, post_g_ref, o_ref = refs[len(att_refs):]
    if n_patterns > 1:
        outs = [att_refs[2 * p][...] for p in range(n_patterns)]
        lses = [att_refs[2 * p + 1][...] for p in range(n_patterns)]
        mx = functools.reduce(jnp.maximum, lses)
        ws = [jnp.exp(l - mx) for l in lses]
        num = functools.reduce(lambda a, b: a + b, [w * o for w, o in zip(ws, outs)])
        att = num / functools.reduce(lambda a, b: a + b, ws)
    else:
        att = att_refs[0][...]
    oa = _rms(att, attn_g_ref[...]).astype(BF16)
    m = _dot(oa, w_ref[:A_WIDTH, :]) + _dot(ob_ref[...].astype(BF16), w_ref[A_WIDTH:, :])
    o_ref[...] = h_ref[...] + _rms(m, post_g_ref[...])


def _mix_out(att_parts, ob, h, attn_g, w_out, post_g, layer, tm):
    m, d = h.shape
    n_patterns = len(att_parts) // 2 if len(att_parts) > 1 else 1
    half = lambda: pl.BlockSpec((tm, A_WIDTH), lambda i: (i, 0))
    return pl.pallas_call(
        functools.partial(_mix_out_kernel, n_patterns=n_patterns),
        grid=(m // tm,),
        in_specs=[half() for _ in att_parts] + [
            half(),
            pl.BlockSpec((tm, d), lambda i: (i, 0)),
            pl.BlockSpec((None, 1, A_WIDTH), lambda i: (layer, 0, 0)),
            pl.BlockSpec((None, A_WIDTH + B_WIDTH, d), lambda i: (layer, 0, 0)),
            pl.BlockSpec((None, 1, d), lambda i: (layer, 0, 0)),
        ],
        out_specs=pl.BlockSpec((tm, d), lambda i: (i, 0)),
        out_shape=jax.ShapeDtypeStruct((m, d), F32),
        compiler_params=_cparams(1),
        name="mix_out",
    )(*att_parts, ob, h, attn_g, w_out, post_g)


def kernel(x_prompt, x_sample, cache_attn_k, cache_attn_v, state_hgrn, ff1_pre_g, ff1_w_gate, ff1_w_up, ff1_w_down, ff1_post_g, mix_pre_g, w_in, attn_norm_g, hgrn_lb_logits, hgrn_norm_g, w_out, mix_post_g, ff2_pre_g, ff2_w_gate, ff2_w_up, ff2_w_down, ff2_post_g):
    batch, seq, d = x_prompt.shape
    dec_batch, dec_seq, _ = x_sample.shape
    depth = w_in.shape[0]
    n_cache = cache_attn_k.shape[2]
    keep = min(DILATIONS[-1] * WINDOW_STEPS, seq)

    row = lambda g: g.reshape(depth, 1, g.shape[-1])
    bf = lambda w: w.astype(BF16)
    ff1 = (row(ff1_pre_g), bf(ff1_w_gate), bf(ff1_w_up), bf(ff1_w_down), row(ff1_post_g))
    ff2 = (row(ff2_pre_g), bf(ff2_w_gate), bf(ff2_w_up), bf(ff2_w_down), row(ff2_post_g))
    mix_pre_g, attn_norm_g, hgrn_norm_g, mix_post_g = map(row, (mix_pre_g, attn_norm_g, hgrn_norm_g, mix_post_g))
    w_in16, w_out16 = bf(w_in), bf(w_out)
    cache_k = cache_attn_k.reshape(depth, dec_batch, n_cache, A_WIDTH)
    cache_v = cache_attn_v.reshape(depth, dec_batch, n_cache, A_WIDTH)
    zero_state = jnp.zeros((batch, B_HEADS, B_DIM, B_DIM), F32)

    hp = x_prompt.reshape(batch * seq, d)
    hs = x_sample.reshape(dec_batch * dec_seq, d)
    tm_p, tm_s, tf = 1024, dec_batch * dec_seq, 256
    outs = {k: [] for k in ("kp", "vp", "sp", "ks", "vs", "ss")}
    for l in range(depth):
        hp = _ffn(hp, *ff1, l, tm_p, tf)
        zp = _mix_in(hp, mix_pre_g, w_in16, l, tm_p)
        att = []
        for dil in DILATIONS:
            att.extend(_attn_prompt(zp, batch, seq, dil))
        ob, sp = _hgrn(zp, hgrn_lb_logits, hgrn_norm_g, zero_state, l, batch, seq,
                       2 * HGRN_CHUNK, HGRN_CHUNK, HGRN_SUB)
        hp = _mix_out(att, ob, hp, attn_norm_g, w_out16, mix_post_g, l, 512)
        hp = _ffn(hp, *ff2, l, tm_p, tf)
        zp3 = zp.reshape(batch, seq, N_SPLITS, A_HEADS, A_HEAD_DIM)
        outs["kp"].append(zp3[:, seq - keep:, 1])
        outs["vp"].append(zp3[:, seq - keep:, 2])
        outs["sp"].append(sp)

        hs = _ffn(hs, *ff1, l, tm_s, tf)
        zs = _mix_in(hs, mix_pre_g, w_in16, l, tm_s)
        att_s = _attn_sample(zs, cache_k, cache_v, l, dec_batch, dec_seq)
        ob_s, ss = _hgrn(zs, hgrn_lb_logits, hgrn_norm_g, state_hgrn[l], l, dec_batch, dec_seq,
                         dec_seq, dec_seq, dec_seq)
        hs = _mix_out([att_s], ob_s, hs, attn_norm_g, w_out16, mix_post_g, l, tm_s)
        hs = _ffn(hs, *ff2, l, tm_s, tf)
        zs3 = zs.reshape(dec_batch, dec_seq, N_SPLITS, A_HEADS, A_HEAD_DIM)
        outs["ks"].append(zs3[:, :, 1])
        outs["vs"].append(zs3[:, :, 2])
        outs["ss"].append(ss)

    stack = lambda k: jnp.stack(outs[k])
    return (hp.reshape(batch, seq, d), hs.reshape(dec_batch, dec_seq, d),
            stack("kp"), stack("vp"), stack("sp"), stack("ks"), stack("vs"), stack("ss"))
```

```python
import functools
import math

import jax
import jax.numpy as jnp
from jax import lax
from jax.experimental import pallas as pl
from jax.experimental.pallas import tpu as pltpu

F32 = jnp.float32
BF16 = jnp.bfloat16

NORM_EPS = 1e-6
A_HEADS = 8
A_HEAD_DIM = 64
A_WIDTH = A_HEADS * A_HEAD_DIM
A_PAIR = 2 * A_HEAD_DIM
B_HEADS = 4
B_DIM = 128
B_WIDTH = B_HEADS * B_DIM
DILATIONS = (1, 4, 16)
WINDOW_STEPS = 128
N_SPLITS = 7
HGRN_CHUNK = 64
HGRN_SUB = 16

V7X_VMEM_LIMIT_BYTES = 56 * 1024 * 1024


def _cparams(n_grid_axes, parallel_first=True):
    first = "parallel" if parallel_first else "arbitrary"
    sem = (first,) + ("arbitrary",) * (n_grid_axes - 1)
    return pltpu.CompilerParams(dimension_semantics=sem,
                                vmem_limit_bytes=V7X_VMEM_LIMIT_BYTES)


def _rms(x, g):
    return x * lax.rsqrt(jnp.mean(x * x, axis=-1, keepdims=True) + NORM_EPS) * g


def _sigmoid(x):
    return 1.0 / (1.0 + jnp.exp(-x))


def _dot(a, b):
    return jnp.dot(a, b, preferred_element_type=F32)


def _dot_nt(a, b):
    return lax.dot_general(a, b, (((1,), (1,)), ((), ())), preferred_element_type=F32)


def _dot_tn(a, b):
    return lax.dot_general(a, b, (((0,), (0,)), ((), ())), preferred_element_type=F32)


def _ffn_kernel(x_ref, pre_g_ref, wg_ref, wu_ref, wd_ref, post_g_ref, o_ref, xn_ref, acc_ref):
    j = pl.program_id(1)

    @pl.when(j == 0)
    def _():
        xn_ref[...] = _rms(x_ref[...], pre_g_ref[...]).astype(BF16)
        acc_ref[...] = jnp.zeros_like(acc_ref)

    xn = xn_ref[...]
    g = _dot(xn, wg_ref[...])
    u = _dot(xn, wu_ref[...])
    a = (g * _sigmoid(g) * u).astype(BF16)
    acc_ref[...] += _dot(a, wd_ref[...])

    @pl.when(j == pl.num_programs(1) - 1)
    def _():
        o_ref[...] = x_ref[...] + 0.5 * _rms(acc_ref[...], post_g_ref[...])


def _ffn(x, pre_g, wg, wu, wd, post_g, layer, tm, tf):
    m, d = x.shape
    ff = wg.shape[-1]
    return pl.pallas_call(
        _ffn_kernel,
        grid=(m // tm, ff // tf),
        in_specs=[
            pl.BlockSpec((tm, d), lambda i, j: (i, 0)),
            pl.BlockSpec((None, 1, d), lambda i, j: (layer, 0, 0)),
            pl.BlockSpec((None, d, tf), lambda i, j: (layer, 0, j)),
            pl.BlockSpec((None, d, tf), lambda i, j: (layer, 0, j)),
            pl.BlockSpec((None, tf, d), lambda i, j: (layer, j, 0)),
            pl.BlockSpec((None, 1, d), lambda i, j: (layer, 0, 0)),
        ],
        out_specs=pl.BlockSpec((tm, d), lambda i, j: (i, 0)),
        out_shape=jax.ShapeDtypeStruct((m, d), F32),
        scratch_shapes=[pltpu.VMEM((tm, d), BF16), pltpu.VMEM((tm, d), F32)],
        compiler_params=_cparams(2),
        name="ffn",
    )(x, pre_g, wg, wu, wd, post_g)


def _mix_in_kernel(h_ref, g_ref, w_ref, z_ref, hn_ref):
    @pl.when(pl.program_id(1) == 0)
    def _():
        hn_ref[...] = _rms(h_ref[...], g_ref[...]).astype(BF16)

    z_ref[...] = _dot(hn_ref[...], w_ref[...])


def _mix_in(h, g, w_in, layer, tm):
    m, d = h.shape
    n = w_in.shape[-1]
    tn = A_WIDTH
    return pl.pallas_call(
        _mix_in_kernel,
        grid=(m // tm, n // tn),
        in_specs=[
            pl.BlockSpec((tm, d), lambda i, j: (i, 0)),
            pl.BlockSpec((None, 1, d), lambda i, j: (layer, 0, 0)),
            pl.BlockSpec((None, d, tn), lambda i, j: (layer, 0, j)),
        ],
        out_specs=pl.BlockSpec((tm, tn), lambda i, j: (i, j)),
        out_shape=jax.ShapeDtypeStruct((m, n), F32),
        scratch_shapes=[pltpu.VMEM((tm, d), BF16)],
        compiler_params=_cparams(2),
        name="mix_in",
    )(h, g, w_in)


def _attn_prompt_kernel(q_ref, k_ref, v_ref, o_ref, m_ref, l_ref, acc_ref):
    w = WINDOW_STEPS
    seq = q_ref.shape[0]
    first_head = lax.broadcasted_iota(jnp.int32, (w, A_PAIR), 1) < A_HEAD_DIM
    qi = lax.broadcasted_iota(jnp.int32, (2 * w, 2 * w), 0) & (w - 1)
    ki = lax.broadcasted_iota(jnp.int32, (2 * w, 2 * w), 1)
    dist = w + qi - ki
    band = (dist >= 0) & (dist <= w)
    own_block = ki >= w
    scale = 1.0 / math.sqrt(A_HEAD_DIM)
    pick = lambda x: jnp.where(first_head, x[:w], x[w:])

    for p_idx, dil in enumerate(DILATIONS):
        span = w * dil
        shift = dil.bit_length() - 1
        last = p_idx == len(DILATIONS) - 1

        def rows_at(start, dil=dil):
            return pl.ds(pl.multiple_of(start, w), w) if dil == 1 else pl.ds(start, w, stride=dil)

        def body(idx, carry, dil=dil, span=span, shift=shift, p_idx=p_idx, last=last, rows_at=rows_at):
            r = idx & (dil - 1)
            c = idx >> shift
            rows = rows_at(c * span + r)
            prev = rows_at(jnp.maximum(c - 1, 0) * span + r)
            q = q_ref[rows, :] * scale
            q2 = jnp.concatenate([jnp.where(first_head, q, 0.0), jnp.where(first_head, 0.0, q)], axis=0)
            kcat = jnp.concatenate([k_ref[prev, :], k_ref[rows, :]], axis=0).astype(BF16)
            vcat = jnp.concatenate([v_ref[prev, :], v_ref[rows, :]], axis=0).astype(BF16)
            s = _dot_nt(q2.astype(BF16), kcat)
            s = jnp.where(band & (own_block | (c > 0)), s, -jnp.inf)
            m_row = jnp.max(s, axis=-1, keepdims=True)
            p = jnp.exp(s - m_row)
            l_row = jnp.sum(p, axis=-1, keepdims=True)
            pv = _dot(p.astype(BF16), vcat)
            m_b = pick(jnp.broadcast_to(m_row, (2 * w, A_PAIR)))
            l_b = pick(jnp.broadcast_to(l_row, (2 * w, A_PAIR)))
            acc_b = pick(pv)
            if p_idx > 0:
                m_o = m_ref[rows, :]
                m_n = jnp.maximum(m_o, m_b)
                a_o = jnp.exp(m_o - m_n)
                a_b = jnp.exp(m_b - m_n)
                m_b = m_n
                l_b = a_o * l_ref[rows, :] + a_b * l_b
                acc_b = a_o * acc_ref[rows, :] + a_b * acc_b
            if last:
                o_ref[rows, :] = acc_b / l_b
            else:
                m_ref[rows, :] = m_b
                l_ref[rows, :] = l_b
                acc_ref[rows, :] = acc_b
            return carry

        lax.fori_loop(0, seq // w, body, 0)


def _attn_prompt(z, batch, seq):
    assert seq % (DILATIONS[-1] * WINDOW_STEPS) == 0
    zv = z.reshape(batch, seq, N_SPLITS * A_WIDTH)
    n_pairs = A_WIDTH // A_PAIR
    spec = lambda split: pl.BlockSpec((None, seq, A_PAIR), lambda b, p: (b, 0, split * n_pairs + p))
    o = pl.pallas_call(
        _attn_prompt_kernel,
        grid=(batch, n_pairs),
        in_specs=[spec(0), spec(1), spec(2)],
        out_specs=pl.BlockSpec((None, seq, A_PAIR), lambda b, p: (b, 0, p)),
        out_shape=jax.ShapeDtypeStruct((batch, seq, A_WIDTH), F32),
        scratch_shapes=[pltpu.VMEM((seq, A_PAIR), F32)] * 3,
        compiler_params=_cparams(2),
        name="attn_prompt",
    )(zv, zv, zv)
    return o.reshape(batch * seq, A_WIDTH)


def _div2(x, n):
    assert n & (n - 1) == 0
    return x >> (n.bit_length() - 1)


def _mod2(x, n):
    assert n & (n - 1) == 0
    return x & (n - 1)


def _pattern_count(delta):
    cnt = jnp.zeros(delta.shape, F32)
    for dil in DILATIONS:
        hit = (delta >= 0) & ((delta & (dil - 1)) == 0) & (delta <= dil * WINDOW_STEPS)
        cnt = cnt + hit.astype(F32)
    return cnt


def _attn_sample_kernel(q_ref, kn_ref, vn_ref, kfar_ref, knear_ref, vfar_ref, vnear_ref, o_ref,
                        wfar_ref, wnear_ref, wnew_ref, *, n_cache):
    t = q_ref.shape[0]
    hq = A_HEADS * t
    far_groups, far_res = kfar_ref.shape[0], kfar_ref.shape[1]
    near_groups, max_dil = knear_ref.shape[0], knear_ref.shape[1]
    n_far = far_groups * far_res * A_HEADS
    n_near = near_groups * max_dil * A_HEADS
    near_start = n_cache - near_groups * max_dil

    @pl.when(pl.program_id(0) == 0)
    def _():
        def weights(n_cols, key_pos, key_head):
            row = lax.broadcasted_iota(jnp.int32, (hq, n_cols), 0)
            col = lax.broadcasted_iota(jnp.int32, (hq, n_cols), 1)
            cnt = _pattern_count(n_cache + _mod2(row, t) - key_pos(col))
            return jnp.where(key_head(col) == _div2(row, t), cnt, 0.0)

        wfar_ref[...] = weights(
            n_far,
            lambda col: _div2(col, A_HEADS * far_res) * max_dil + _mod2(_div2(col, A_HEADS), far_res),
            lambda col: _mod2(col, A_HEADS))
        wnear_ref[...] = weights(n_near, lambda col: near_start + _div2(col, A_HEADS),
                                 lambda col: _mod2(col, A_HEADS))
        wnew_ref[...] = weights(hq, lambda col: n_cache + _mod2(col, t), lambda col: _div2(col, t))

    def by_head(ref, mul=None):
        parts = [ref[:, h * A_HEAD_DIM:(h + 1) * A_HEAD_DIM] for h in range(A_HEADS)]
        x = jnp.concatenate(parts, axis=0)
        return (x if mul is None else x * mul).astype(BF16)

    q = by_head(q_ref, 1.0 / math.sqrt(A_HEAD_DIM))
    keys = (kfar_ref[...].reshape(n_far, A_HEAD_DIM).astype(BF16),
            knear_ref[...].reshape(n_near, A_HEAD_DIM).astype(BF16),
            by_head(kn_ref))
    vals = (vfar_ref[...].reshape(n_far, A_HEAD_DIM).astype(BF16),
            vnear_ref[...].reshape(n_near, A_HEAD_DIM).astype(BF16),
            by_head(vn_ref))
    wts = (wfar_ref[...], wnear_ref[...], wnew_ref[...])
    scores = [jnp.where(wt > 0, _dot_nt(q, k), -jnp.inf) for wt, k in zip(wts, keys)]
    m = functools.reduce(jnp.maximum, [jnp.max(s, axis=-1, keepdims=True) for s in scores])
    probs = [wt * jnp.exp(s - m) for wt, s in zip(wts, scores)]
    l = functools.reduce(lambda a, b: a + b, [jnp.sum(p, axis=-1, keepdims=True) for p in probs])
    acc = functools.reduce(lambda a, b: a + b, [_dot(p.astype(BF16), v) for p, v in zip(probs, vals)])
    out = acc / l
    for h in range(A_HEADS):
        o_ref[:, h * A_HEAD_DIM:(h + 1) * A_HEAD_DIM] = out[h * t:(h + 1) * t, :]


def _attn_sample(z, cache_k, cache_v, layer, batch, t):
    depth, _, n_cache = cache_k.shape[:3]
    max_dil = DILATIONS[-1]
    near = DILATIONS[-2] * WINDOW_STEPS
    assert t <= max_dil and n_cache % max_dil == 0 and near % max_dil == 0 and n_cache >= near
    assert (n_cache - near) % near == 0
    groups = n_cache // max_dil
    near_groups = near // max_dil
    far_groups = groups - near_groups
    hq = A_HEADS * t
    zv = z.reshape(batch, t, N_SPLITS * A_WIDTH)
    ck = cache_k.reshape(depth, batch, groups, max_dil, A_HEADS, A_HEAD_DIM)
    cv = cache_v.reshape(depth, batch, groups, max_dil, A_HEADS, A_HEAD_DIM)
    zspec = lambda col: pl.BlockSpec((None, t, A_WIDTH), lambda b: (b, 0, col))
    far = pl.BlockSpec((None, None, far_groups, t, A_HEADS, A_HEAD_DIM), lambda b: (layer, b, 0, 0, 0, 0))
    nearspec = pl.BlockSpec((None, None, near_groups, max_dil, A_HEADS, A_HEAD_DIM),
                            lambda b: (layer, b, far_groups // near_groups, 0, 0, 0))
    o = pl.pallas_call(
        functools.partial(_attn_sample_kernel, n_cache=n_cache),
        grid=(batch,),
        in_specs=[zspec(0), zspec(1), zspec(2), far, nearspec, far, nearspec],
        out_specs=pl.BlockSpec((None, t, A_WIDTH), lambda b: (b, 0, 0)),
        out_shape=jax.ShapeDtypeStruct((batch, t, A_WIDTH), F32),
        scratch_shapes=[pltpu.VMEM((hq, far_groups * t * A_HEADS), F32),
                        pltpu.VMEM((hq, near * A_HEADS), F32),
                        pltpu.VMEM((hq, hq), F32)],
        compiler_params=_cparams(1, parallel_first=False),
        name="attn_sample",
    )(zv, zv, zv, ck, ck, cv, cv)
    return o.reshape(batch * t, A_WIDTH)


def _kv_format_kernel(*refs, depth):
    k_refs, v_refs = refs[:depth], refs[depth:2 * depth]
    ko_ref, vo_ref = refs[2 * depth:]
    for lyr in range(depth):
        @pl.when(pl.program_id(0) == lyr)
        def _(lyr=lyr):
            for src, dst in ((k_refs[lyr], ko_ref), (v_refs[lyr], vo_ref)):
                for h in range(A_HEADS):
                    dst[:, h, :] = src[:, h * A_HEAD_DIM:(h + 1) * A_HEAD_DIM]


def _kv_format(zs, batch, seq, keep, tm):
    depth = len(zs)
    zvs = [z.reshape(batch, seq, N_SPLITS * A_WIDTH) for z in zs]
    t0, nt = (seq - keep) // tm, keep // tm

    def in_spec(lyr, split):
        def index(l, b, t):
            before, after = l < lyr, l > lyr
            bb = jnp.where(before, 0, jnp.where(after, batch - 1, b))
            tt = jnp.where(before, 0, jnp.where(after, nt - 1, t))
            return (bb, t0 + tt, split)
        return pl.BlockSpec((None, tm, A_WIDTH), index)

    out_spec = pl.BlockSpec((None, None, tm, A_HEADS, A_HEAD_DIM), lambda l, b, t: (l, b, t, 0, 0))
    shape = jax.ShapeDtypeStruct((depth, batch, keep, A_HEADS, A_HEAD_DIM), F32)
    return pl.pallas_call(
        functools.partial(_kv_format_kernel, depth=depth),
        grid=(depth, batch, nt),
        in_specs=[in_spec(l, 1) for l in range(depth)] + [in_spec(l, 2) for l in range(depth)],
        out_specs=[out_spec, out_spec],
        out_shape=[shape, shape],
        compiler_params=_cparams(3, parallel_first=False),
        name="kv_format",
    )(*zvs, *zvs)


def _hgrn_kernel(xq_ref, xf_ref, xi_ref, xg_ref, lbl_ref, g_ref, s0_ref, o_ref, sfin_ref, st_ref,
                 *, layer, chunk, sub):
    t_id = pl.program_id(1)

    @pl.when(t_id == 0)
    def _():
        for h in range(B_HEADS):
            st_ref[h] = s0_ref[h].T

    logits = lbl_ref[...]
    e = jnp.exp(logits - jnp.max(logits, axis=0, keepdims=True))
    soft = e / jnp.sum(e, axis=0, keepdims=True)
    lb_all = soft[0:1] - soft[0:1]
    for l in range(1, layer + 1):
        lb_all = lb_all + soft[l:l + 1]

    n_chunks = xq_ref.shape[0] // chunk
    n_sub = chunk // sub
    tri = (lax.broadcasted_iota(jnp.int32, (chunk, chunk), 0)
           >= lax.broadcasted_iota(jnp.int32, (chunk, chunk), 1)).astype(BF16)
    row_c = lax.broadcasted_iota(jnp.int32, (chunk, B_DIM), 0)
    row_s = lax.broadcasted_iota(jnp.int32, (sub, B_DIM), 0)
    lane_c = lax.broadcasted_iota(jnp.int32, (sub, chunk), 1)

    for ci in range(n_chunks):
        rows = slice(ci * chunk, (ci + 1) * chunk)
        for h in range(B_HEADS):
            cols = slice(h * B_DIM, (h + 1) * B_DIM)
            xq = xq_ref[rows, cols]
            xg = xg_ref[rows, cols]
            v = xi_ref[rows, cols]
            lb = lb_all[:, cols]
            qh = xq * _sigmoid(xq)
            f = lb + (1.0 - lb) * _sigmoid(xf_ref[rows, cols])
            logf = jnp.log(f)
            kh = 1.0 - f

            hi = logf.astype(BF16)
            r1 = logf - hi.astype(F32)
            mid = r1.astype(BF16)
            lo = (r1 - mid.astype(F32)).astype(BF16)
            gc = _dot(tri, hi) + _dot(tri, mid) + _dot(tri, lo)

            st = st_ref[h]
            o = _dot_nt((qh * jnp.exp(gc)).astype(BF16), st.astype(BF16))

            a_rows = []
            for i in range(n_sub):
                r0 = i * sub
                gi = gc[r0:r0 + sub]
                qi = qh[r0:r0 + sub]
                ki = kh[r0:r0 + sub]
                if i > 0:
                    bnd = gc[r0 - 1:r0]
                    qo = (qi * jnp.exp(gi - bnd)).astype(BF16)
                    kk = (kh * jnp.exp(jnp.where(row_c < r0, bnd - gc, -jnp.inf))).astype(BF16)
                    a_i = _dot_nt(qo, kk)
                else:
                    a_i = jnp.zeros((sub, chunk), F32)
                for s in range(sub):
                    dec = jnp.exp(jnp.where(row_s >= s, gi - gi[s:s + 1], -jnp.inf))
                    col = jnp.sum(qi * ki[s:s + 1] * dec, axis=-1, keepdims=True)
                    a_i = jnp.where(lane_c == r0 + s, col, a_i)
                a_rows.append(a_i)
            a = a_rows[0] if n_sub == 1 else jnp.concatenate(a_rows, axis=0)
            v16 = v.astype(BF16)
            o = o + _dot(a.astype(BF16), v16)

            g_last = gc[chunk - 1:chunk]
            kd = (kh * jnp.exp(g_last - gc)).astype(BF16)
            st_ref[h] = jnp.exp(g_last) * st + _dot_tn(v16, kd)

            o_ref[rows, cols] = _rms(o, g_ref[...]) * (xg * _sigmoid(xg))

    @pl.when(t_id == pl.num_programs(1) - 1)
    def _():
        for h in range(B_HEADS):
            sfin_ref[h] = st_ref[h].T


def _hgrn(z, lb_logits, norm_g, s0, layer, batch, seq, tt, chunk, sub):
    zv = z.reshape(batch, seq, N_SPLITS * A_WIDTH)
    zspec = lambda col: pl.BlockSpec((None, tt, B_WIDTH), lambda b, t: (b, t, col))
    sspec = pl.BlockSpec((None, B_HEADS, B_DIM, B_DIM), lambda b, t: (b, 0, 0, 0))
    depth = lb_logits.shape[0]
    o, sfin = pl.pallas_call(
        functools.partial(_hgrn_kernel, layer=layer, chunk=chunk, sub=sub),
        grid=(batch, seq // tt),
        in_specs=[zspec(3), zspec(4), zspec(5), zspec(6),
                  pl.BlockSpec((depth, B_WIDTH), lambda b, t: (0, 0)),
                  pl.BlockSpec((None, 1, B_DIM), lambda b, t: (layer, 0, 0)),
                  sspec],
        out_specs=[pl.BlockSpec((None, tt, B_WIDTH), lambda b, t: (b, t, 0)), sspec],
        out_shape=[jax.ShapeDtypeStruct((batch, seq, B_WIDTH), F32),
                   jax.ShapeDtypeStruct((batch, B_HEADS, B_DIM, B_DIM), F32)],
        scratch_shapes=[pltpu.VMEM((B_HEADS, B_DIM, B_DIM), F32)],
        compiler_params=_cparams(2),
        name="hgrn",
    )(zv, zv, zv, zv, lb_logits, norm_g, s0)
    return o.reshape(batch * seq, B_WIDTH), sfin


def _mix_out_kernel(att_ref, ob_ref, h_ref, attn_g_ref, w_ref, post_g_ref, o_ref):
    oa = _rms(att_ref[...], attn_g_ref[...]).astype(BF16)
    m = _dot(oa, w_ref[:A_WIDTH, :]) + _dot(ob_ref[...].astype(BF16), w_ref[A_WIDTH:, :])
    o_ref[...] = h_ref[...] + _rms(m, post_g_ref[...])


def _mix_out(att, ob, h, attn_g, w_out, post_g, layer, tm):
    m, d = h.shape
    half = pl.BlockSpec((tm, A_WIDTH), lambda i: (i, 0))
    return pl.pallas_call(
        _mix_out_kernel,
        grid=(m // tm,),
        in_specs=[
            half, half,
            pl.BlockSpec((tm, d), lambda i: (i, 0)),
            pl.BlockSpec((None, 1, A_WIDTH), lambda i: (layer, 0, 0)),
            pl.BlockSpec((None, A_WIDTH + B_WIDTH, d), lambda i: (layer, 0, 0)),
            pl.BlockSpec((None, 1, d), lambda i: (layer, 0, 0)),
        ],
        out_specs=pl.BlockSpec((tm, d), lambda i: (i, 0)),
        out_shape=jax.ShapeDtypeStruct((m, d), F32),
        compiler_params=_cparams(1),
        name="mix_out",
    )(att, ob, h, attn_g, w_out, post_g)


def kernel(x_prompt, x_sample, cache_attn_k, cache_attn_v, state_hgrn, ff1_pre_g, ff1_w_gate, ff1_w_up, ff1_w_down, ff1_post_g, mix_pre_g, w_in, attn_norm_g, hgrn_lb_logits, hgrn_norm_g, w_out, mix_post_g, ff2_pre_g, ff2_w_gate, ff2_w_up, ff2_w_down, ff2_post_g):
    batch, seq, d = x_prompt.shape
    dec_batch, dec_seq, _ = x_sample.shape
    depth = w_in.shape[0]
    keep = min(DILATIONS[-1] * WINDOW_STEPS, seq)

    row = lambda g: g.reshape(depth, 1, g.shape[-1])
    bf = lambda w: w.astype(BF16)
    ff1 = (row(ff1_pre_g), bf(ff1_w_gate), bf(ff1_w_up), bf(ff1_w_down), row(ff1_post_g))
    ff2 = (row(ff2_pre_g), bf(ff2_w_gate), bf(ff2_w_up), bf(ff2_w_down), row(ff2_post_g))
    mix_pre_g, attn_norm_g, hgrn_norm_g, mix_post_g = map(row, (mix_pre_g, attn_norm_g, hgrn_norm_g, mix_post_g))
    w_in16, w_out16 = bf(w_in), bf(w_out)
    zero_state = jnp.zeros((batch, B_HEADS, B_DIM, B_DIM), F32)

    hp = x_prompt.reshape(batch * seq, d)
    hs = x_sample.reshape(dec_batch * dec_seq, d)
    tm_p, tm_s, tf = 1024, dec_batch * dec_seq, 256
    zps = []
    outs = {k: [] for k in ("sp", "ks", "vs", "ss")}
    for l in range(depth):
        hp = _ffn(hp, *ff1, l, tm_p, tf)
        zp = _mix_in(hp, mix_pre_g, w_in16, l, tm_p)
        att = _attn_prompt(zp, batch, seq)
        ob, sp = _hgrn(zp, hgrn_lb_logits, hgrn_norm_g, zero_state, l, batch, seq,
                       2 * HGRN_CHUNK, HGRN_CHUNK, HGRN_SUB)
        hp = _mix_out(att, ob, hp, attn_norm_g, w_out16, mix_post_g, l, 512)
        hp = _ffn(hp, *ff2, l, tm_p, tf)
        zps.append(zp)
        outs["sp"].append(sp)

        hs = _ffn(hs, *ff1, l, tm_s, tf)
        zs = _mix_in(hs, mix_pre_g, w_in16, l, tm_s)
        att_s = _attn_sample(zs, cache_attn_k, cache_attn_v, l, dec_batch, dec_seq)
        ob_s, ss = _hgrn(zs, hgrn_lb_logits, hgrn_norm_g, state_hgrn[l], l, dec_batch, dec_seq,
                         dec_seq, dec_seq, dec_seq)
        hs = _mix_out(att_s, ob_s, hs, attn_norm_g, w_out16, mix_post_g, l, tm_s)
        hs = _ffn(hs, *ff2, l, tm_s, tf)
        zs3 = zs.reshape(dec_batch, dec_seq, N_SPLITS, A_HEADS, A_HEAD_DIM)
        outs["ks"].append(zs3[:, :, 1])
        outs["vs"].append(zs3[:, :, 2])
        outs["ss"].append(ss)

    kp, vp = _kv_format(zps, batch, seq, keep, 512)
    stack = lambda k: jnp.stack(outs[k])
    return (hp.reshape(batch, seq, d), hs.reshape(dec_batch, dec_seq, d),
            kp, vp, stack("sp"), stack("ks"), stack("vs"), stack("ss"))
```

```python
import functools
import math

import jax
import jax.numpy as jnp
from jax import lax
from jax.experimental import pallas as pl
from jax.experimental.pallas import tpu as pltpu

F32 = jnp.float32
BF16 = jnp.bfloat16

NORM_EPS = 1e-6
A_HEADS = 8
A_HEAD_DIM = 64
A_WIDTH = A_HEADS * A_HEAD_DIM
A_PAIR = 2 * A_HEAD_DIM
B_HEADS = 4
B_DIM = 128
B_WIDTH = B_HEADS * B_DIM
DILATIONS = (1, 4, 16)
WINDOW_STEPS = 128
N_SPLITS = 7
HGRN_CHUNK = 64
HGRN_SUB = 16

V7X_VMEM_LIMIT_BYTES = 56 * 1024 * 1024


def _cparams(n_grid_axes, parallel_first=True):
    first = "parallel" if parallel_first else "arbitrary"
    sem = (first,) + ("arbitrary",) * (n_grid_axes - 1)
    return pltpu.CompilerParams(dimension_semantics=sem,
                                vmem_limit_bytes=V7X_VMEM_LIMIT_BYTES)


def _rms(x, g):
    return x * lax.rsqrt(jnp.mean(x * x, axis=-1, keepdims=True) + NORM_EPS) * g


def _sigmoid(x):
    return 1.0 / (1.0 + jnp.exp(-x))


def _dot(a, b):
    return jnp.dot(a, b, preferred_element_type=F32)


def _dot_nt(a, b):
    return lax.dot_general(a, b, (((1,), (1,)), ((), ())), preferred_element_type=F32)


def _dot_tn(a, b):
    return lax.dot_general(a, b, (((0,), (0,)), ((), ())), preferred_element_type=F32)


def _ffn_kernel(x_ref, pre_g_ref, wg_ref, wu_ref, wd_ref, post_g_ref, o_ref, xn_ref, acc_ref):
    j = pl.program_id(1)

    @pl.when(j == 0)
    def _():
        xn_ref[...] = _rms(x_ref[...], pre_g_ref[...]).astype(BF16)
        acc_ref[...] = jnp.zeros_like(acc_ref)

    xn = xn_ref[...]
    g = _dot(xn, wg_ref[...])
    u = _dot(xn, wu_ref[...])
    a = (g * _sigmoid(g) * u).astype(BF16)
    acc_ref[...] += _dot(a, wd_ref[...])

    @pl.when(j == pl.num_programs(1) - 1)
    def _():
        o_ref[...] = x_ref[...] + 0.5 * _rms(acc_ref[...], post_g_ref[...])


def _ffn(x, pre_g, wg, wu, wd, post_g, layer, tm, tf):
    m, d = x.shape
    ff = wg.shape[-1]
    return pl.pallas_call(
        _ffn_kernel,
        grid=(m // tm, ff // tf),
        in_specs=[
            pl.BlockSpec((tm, d), lambda i, j: (i, 0)),
            pl.BlockSpec((None, 1, d), lambda i, j: (layer, 0, 0)),
            pl.BlockSpec((None, d, tf), lambda i, j: (layer, 0, j)),
            pl.BlockSpec((None, d, tf), lambda i, j: (layer, 0, j)),
            pl.BlockSpec((None, tf, d), lambda i, j: (layer, j, 0)),
            pl.BlockSpec((None, 1, d), lambda i, j: (layer, 0, 0)),
        ],
        out_specs=pl.BlockSpec((tm, d), lambda i, j: (i, 0)),
        out_shape=jax.ShapeDtypeStruct((m, d), F32),
        scratch_shapes=[pltpu.VMEM((tm, d), BF16), pltpu.VMEM((tm, d), F32)],
        compiler_params=_cparams(2),
        name="ffn",
    )(x, pre_g, wg, wu, wd, post_g)


def _mix_in_kernel(h_ref, g_ref, w_ref, z_ref, hn_ref):
    @pl.when(pl.program_id(1) == 0)
    def _():
        hn_ref[...] = _rms(h_ref[...], g_ref[...]).astype(BF16)

    z_ref[...] = _dot(hn_ref[...], w_ref[...])


def _mix_in(h, g, w_in, layer, tm):
    m, d = h.shape
    n = w_in.shape[-1]
    tn = A_WIDTH
    return pl.pallas_call(
        _mix_in_kernel,
        grid=(m // tm, n // tn),
        in_specs=[
            pl.BlockSpec((tm, d), lambda i, j: (i, 0)),
            pl.BlockSpec((None, 1, d), lambda i, j: (layer, 0, 0)),
            pl.BlockSpec((None, d, tn), lambda i, j: (layer, 0, j)),
        ],
        out_specs=pl.BlockSpec((tm, tn), lambda i, j: (i, j)),
        out_shape=jax.ShapeDtypeStruct((m, n), F32),
        scratch_shapes=[pltpu.VMEM((tm, d), BF16)],
        compiler_params=_cparams(2),
        name="mix_in",
    )(h, g, w_in)


def _attn_block(q, k_prev, k_own, v_prev, v_own, bias, first_head):
    w = q.shape[0]
    q = q * (1.0 / math.sqrt(A_HEAD_DIM))
    q2 = jnp.concatenate([jnp.where(first_head, q, 0.0), jnp.where(first_head, 0.0, q)], axis=0)
    kcat = jnp.concatenate([k_prev, k_own], axis=0).astype(BF16)
    vcat = jnp.concatenate([v_prev, v_own], axis=0).astype(BF16)
    s = _dot_nt(q2.astype(BF16), kcat) + jnp.concatenate([bias, bias], axis=0)
    m_row = jnp.max(s, axis=-1, keepdims=True)
    p = jnp.exp(s - m_row)
    l_row = jnp.sum(p, axis=-1, keepdims=True)
    pv = _dot(p.astype(BF16), vcat)
    pick = lambda x: jnp.where(first_head, x[:w], x[w:])
    return (pick(jnp.broadcast_to(m_row, (2 * w, A_PAIR))),
            pick(jnp.broadcast_to(l_row, (2 * w, A_PAIR))), pick(pv))


def _attn_merge(m_o, l_o, acc_o, m_b, l_b, acc_b):
    m_n = jnp.maximum(m_o, m_b)
    a_o = jnp.exp(m_o - m_n)
    a_b = jnp.exp(m_b - m_n)
    return m_n, a_o * l_o + a_b * l_b, a_o * acc_o + a_b * acc_b


def _attn_prompt_kernel(q_ref, k_ref, v_ref, o_ref, m_ref, l_ref, acc_ref, q4_ref, k4_ref, v4_ref,
                        m4_ref, l4_ref, acc4_ref, bias_ref, *, unroll):
    w = WINDOW_STEPS
    seq = q_ref.shape[0]
    d1, d2 = DILATIONS[1], DILATIONS[2]
    ratio = d2 // d1
    per_res = seq // d1
    nb1 = per_res // w
    nb2 = seq // (d2 * w)
    first_head = lax.broadcasted_iota(jnp.int32, (w, A_PAIR), 1) < A_HEAD_DIM
    qi = lax.broadcasted_iota(jnp.int32, (w, 2 * w), 0)
    ki = lax.broadcasted_iota(jnp.int32, (w, 2 * w), 1)
    dist = w + qi - ki
    band = (dist >= 0) & (dist <= w)
    bias_ref[0] = jnp.where(band & (ki >= w), 0.0, -jnp.inf)
    bias_ref[1] = jnp.where(band, 0.0, -jnp.inf)

    def pattern0(c, carry):
        rows = pl.ds(pl.multiple_of(c * w, w), w)
        prev = pl.ds(pl.multiple_of(jnp.maximum(c - 1, 0) * w, w), w)
        m_b, l_b, acc_b = _attn_block(q_ref[rows, :], k_ref[prev, :], k_ref[rows, :], v_ref[prev, :],
                                      v_ref[rows, :], bias_ref[jnp.minimum(c, 1)], first_head)
        m_ref[rows, :] = m_b
        l_ref[rows, :] = l_b
        acc_ref[rows, :] = acc_b
        return carry

    lax.fori_loop(0, seq // w, pattern0, 0, unroll=unroll)

    for r in range(d1):
        dst = pl.ds(r * per_res, per_res)
        src = pl.ds(r, per_res, stride=d1)
        for nat, grouped in ((q_ref, q4_ref), (k_ref, k4_ref), (v_ref, v4_ref),
                             (m_ref, m4_ref), (l_ref, l4_ref), (acc_ref, acc4_ref)):
            grouped[dst, :] = nat[src, :]

    def pattern1(idx, carry):
        has_prev = ((idx & (nb1 - 1)) != 0).astype(jnp.int32)
        rows = pl.ds(pl.multiple_of(idx * w, w), w)
        prev = pl.ds(pl.multiple_of((idx - has_prev) * w, w), w)
        blk = _attn_block(q4_ref[rows, :], k4_ref[prev, :], k4_ref[rows, :], v4_ref[prev, :],
                          v4_ref[rows, :], bias_ref[has_prev], first_head)
        m_n, l_n, acc_n = _attn_merge(m4_ref[rows, :], l4_ref[rows, :], acc4_ref[rows, :], *blk)
        m4_ref[rows, :] = m_n
        l4_ref[rows, :] = l_n
        acc4_ref[rows, :] = acc_n
        return carry

    assert nb1 & (nb1 - 1) == 0 and nb2 & (nb2 - 1) == 0 and ratio & (ratio - 1) == 0
    lax.fori_loop(0, seq // w, pattern1, 0, unroll=unroll)

    def pattern2(idx, carry):
        a = idx & (ratio - 1)
        c = (idx >> (ratio.bit_length() - 1)) & (nb2 - 1)
        r = idx >> ((ratio * nb2).bit_length() - 1)
        has_prev = (c != 0).astype(jnp.int32)
        start = r * per_res + c * (w * ratio) + a
        rows = pl.ds(start, w, stride=ratio)
        prev = pl.ds(start - has_prev * (w * ratio), w, stride=ratio)
        blk = _attn_block(q4_ref[rows, :], k4_ref[prev, :], k4_ref[rows, :], v4_ref[prev, :],
                          v4_ref[rows, :], bias_ref[has_prev], first_head)
        _, l_n, acc_n = _attn_merge(m4_ref[rows, :], l4_ref[rows, :], acc4_ref[rows, :], *blk)
        acc_ref[rows, :] = acc_n / l_n
        return carry

    lax.fori_loop(0, seq // w, pattern2, 0, unroll=unroll)

    for r in range(d1):
        o_ref[pl.ds(r, per_res, stride=d1), :] = acc_ref[pl.ds(r * per_res, per_res), :]


def _attn_prompt(z, batch, seq, unroll=8):
    assert DILATIONS[0] == 1 and len(DILATIONS) == 3 and DILATIONS[2] % DILATIONS[1] == 0
    assert seq % (DILATIONS[-1] * WINDOW_STEPS) == 0
    zv = z.reshape(batch, seq, N_SPLITS * A_WIDTH)
    n_pairs = A_WIDTH // A_PAIR
    spec = lambda split: pl.BlockSpec((None, seq, A_PAIR), lambda b, p: (b, 0, split * n_pairs + p))
    o = pl.pallas_call(
        functools.partial(_attn_prompt_kernel, unroll=unroll),
        grid=(batch, n_pairs),
        in_specs=[spec(0), spec(1), spec(2)],
        out_specs=pl.BlockSpec((None, seq, A_PAIR), lambda b, p: (b, 0, p)),
        out_shape=jax.ShapeDtypeStruct((batch, seq, A_WIDTH), F32),
        scratch_shapes=[pltpu.VMEM((seq, A_PAIR), F32)] * 9
        + [pltpu.VMEM((2, WINDOW_STEPS, 2 * WINDOW_STEPS), F32)],
        compiler_params=_cparams(2),
        name="attn_prompt",
    )(zv, zv, zv)
    return o.reshape(batch * seq, A_WIDTH)


def _pattern_count(delta):
    cnt = jnp.zeros(delta.shape, F32)
    for dil in DILATIONS:
        hit = (delta >= 0) & ((delta & (dil - 1)) == 0) & (delta <= dil * WINDOW_STEPS)
        cnt = cnt + hit.astype(F32)
    return cnt


def _attn_sample_kernel(q_ref, kn_ref, vn_ref, kt_ref, vt_ref, o_ref):
    t = q_ref.shape[0]
    hq = A_HEADS * t
    n_cache = kt_ref.shape[-1]
    assert t & (t - 1) == 0
    t_shift = t.bit_length() - 1

    def weights(n_cols, first_pos):
        query = lax.broadcasted_iota(jnp.int32, (hq, n_cols), 0) & (t - 1)
        key = first_pos + lax.broadcasted_iota(jnp.int32, (hq, n_cols), 1)
        return _pattern_count(n_cache + query - key)

    w_cache = weights(n_cache, 0)
    w_new = weights(t, n_cache)

    row_head = lax.broadcasted_iota(jnp.int32, (hq, A_WIDTH), 0) >> t_shift
    lane_head = lax.broadcasted_iota(jnp.int32, (hq, A_WIDTH), 1) >> (A_HEAD_DIM.bit_length() - 1)
    own_head = row_head == lane_head
    q = q_ref[...] * (1.0 / math.sqrt(A_HEAD_DIM))
    q_bd = jnp.where(own_head, jnp.concatenate([q] * A_HEADS, axis=0), 0.0).astype(BF16)

    kt = kt_ref[...].reshape(A_WIDTH, n_cache).astype(BF16)
    vt = vt_ref[...].reshape(A_WIDTH, n_cache).astype(BF16)
    s_c = jnp.where(w_cache > 0, _dot(q_bd, kt), -jnp.inf)
    s_n = jnp.where(w_new > 0, _dot_nt(q_bd, kn_ref[...].astype(BF16)), -jnp.inf)
    m = jnp.maximum(jnp.max(s_c, axis=-1, keepdims=True), jnp.max(s_n, axis=-1, keepdims=True))
    p_c = w_cache * jnp.exp(s_c - m)
    p_n = w_new * jnp.exp(s_n - m)
    l = jnp.sum(p_c, axis=-1, keepdims=True) + jnp.sum(p_n, axis=-1, keepdims=True)
    r = (_dot_nt(p_c.astype(BF16), vt) + _dot(p_n.astype(BF16), vn_ref[...].astype(BF16))) / l
    r = jnp.where(own_head, r, 0.0)
    out = r[0:t]
    for h in range(1, A_HEADS):
        out = out + r[h * t:(h + 1) * t]
    o_ref[...] = out


def _attn_sample(z, cache_k, cache_v, layer, batch, t):
    n_cache = cache_k.shape[2]
    zv = z.reshape(batch, t, N_SPLITS * A_WIDTH)
    ckt = jnp.transpose(cache_k, (0, 1, 3, 4, 2))
    cvt = jnp.transpose(cache_v, (0, 1, 3, 4, 2))
    zspec = lambda col: pl.BlockSpec((None, t, A_WIDTH), lambda b: (b, 0, col))
    cspec = pl.BlockSpec((None, None, A_HEADS, A_HEAD_DIM, n_cache), lambda b: (layer, b, 0, 0, 0))
    o = pl.pallas_call(
        _attn_sample_kernel,
        grid=(batch,),
        in_specs=[zspec(0), zspec(1), zspec(2), cspec, cspec],
        out_specs=pl.BlockSpec((None, t, A_WIDTH), lambda b: (b, 0, 0)),
        out_shape=jax.ShapeDtypeStruct((batch, t, A_WIDTH), F32),
        compiler_params=_cparams(1),
        name="attn_sample",
    )(zv, zv, zv, ckt, cvt)
    return o.reshape(batch * t, A_WIDTH)


def _kv_format_kernel(*refs, depth):
    k_refs, v_refs = refs[:depth], refs[depth:2 * depth]
    ko_ref, vo_ref = refs[2 * depth:]
    tm = ko_ref.shape[-1]
    for lyr in range(depth):
        @pl.when(pl.program_id(0) == lyr)
        def _(lyr=lyr):
            for src, dst in ((k_refs[lyr], ko_ref), (v_refs[lyr], vo_ref)):
                dst[...] = src[...].T.reshape(A_HEADS, A_HEAD_DIM, tm)


def _kv_format(zs, batch, seq, keep, tm):
    depth = len(zs)
    zvs = [z.reshape(batch, seq, N_SPLITS * A_WIDTH) for z in zs]
    t0, nt = (seq - keep) // tm, keep // tm

    def in_spec(lyr, split):
        def index(l, b, t):
            before, after = l < lyr, l > lyr
            bb = jnp.where(before, 0, jnp.where(after, batch - 1, b))
            tt = jnp.where(before, 0, jnp.where(after, nt - 1, t))
            return (bb, t0 + tt, split)
        return pl.BlockSpec((None, tm, A_WIDTH), index)

    out_spec = pl.BlockSpec((None, None, A_HEADS, A_HEAD_DIM, tm), lambda l, b, t: (l, b, 0, 0, t))
    shape = jax.ShapeDtypeStruct((depth, batch, A_HEADS, A_HEAD_DIM, keep), F32)
    kt, vt = pl.pallas_call(
        functools.partial(_kv_format_kernel, depth=depth),
        grid=(depth, batch, nt),
        in_specs=[in_spec(l, 1) for l in range(depth)] + [in_spec(l, 2) for l in range(depth)],
        out_specs=[out_spec, out_spec],
        out_shape=[shape, shape],
        compiler_params=_cparams(3, parallel_first=False),
        name="kv_format",
    )(*zvs, *zvs)
    return jnp.transpose(kt, (0, 1, 4, 2, 3)), jnp.transpose(vt, (0, 1, 4, 2, 3))


def _hgrn_kernel(xq_ref, xf_ref, xi_ref, xg_ref, lbl_ref, g_ref, s0_ref, o_ref, sfin_ref, st_ref,
                 *, layer, chunk, sub):
    t_id = pl.program_id(1)

    @pl.when(t_id == 0)
    def _():
        for h in range(B_HEADS):
            st_ref[h] = s0_ref[h].T

    logits = lbl_ref[...]
    e = jnp.exp(logits - jnp.max(logits, axis=0, keepdims=True))
    soft = e / jnp.sum(e, axis=0, keepdims=True)
    lb_all = soft[0:1] - soft[0:1]
    for l in range(1, layer + 1):
        lb_all = lb_all + soft[l:l + 1]

    n_chunks = xq_ref.shape[0] // chunk
    n_sub = chunk // sub
    tri = (lax.broadcasted_iota(jnp.int32, (chunk, chunk), 0)
           >= lax.broadcasted_iota(jnp.int32, (chunk, chunk), 1)).astype(BF16)
    row_c = lax.broadcasted_iota(jnp.int32, (chunk, B_DIM), 0)
    row_s = lax.broadcasted_iota(jnp.int32, (sub, B_DIM), 0)
    lane_c = lax.broadcasted_iota(jnp.int32, (sub, chunk), 1)

    for ci in range(n_chunks):
        rows = slice(ci * chunk, (ci + 1) * chunk)
        for h in range(B_HEADS):
            cols = slice(h * B_DIM, (h + 1) * B_DIM)
            xq = xq_ref[rows, cols]
            xg = xg_ref[rows, cols]
            v = xi_ref[rows, cols]
            lb = lb_all[:, cols]
            qh = xq * _sigmoid(xq)
            f = lb + (1.0 - lb) * _sigmoid(xf_ref[rows, cols])
            logf = jnp.log(f)
            kh = 1.0 - f

            hi = logf.astype(BF16)
            r1 = logf - hi.astype(F32)
            mid = r1.astype(BF16)
            lo = (r1 - mid.astype(F32)).astype(BF16)
            gc = _dot(tri, hi) + _dot(tri, mid) + _dot(tri, lo)

            st = st_ref[h]
            o = _dot_nt((qh * jnp.exp(gc)).astype(BF16), st.astype(BF16))

            a_rows = []
            for i in range(n_sub):
                r0 = i * sub
                gi = gc[r0:r0 + sub]
                qi = qh[r0:r0 + sub]
                ki = kh[r0:r0 + sub]
                if i > 0:
                    bnd = gc[r0 - 1:r0]
                    qo = (qi * jnp.exp(gi - bnd)).astype(BF16)
                    kk = (kh * jnp.exp(jnp.where(row_c < r0, bnd - gc, -jnp.inf))).astype(BF16)
                    a_i = _dot_nt(qo, kk)
                else:
                    a_i = jnp.zeros((sub, chunk), F32)
                for s in range(sub):
                    dec = jnp.exp(jnp.where(row_s >= s, gi - gi[s:s + 1], -jnp.inf))
                    col = jnp.sum(qi * ki[s:s + 1] * dec, axis=-1, keepdims=True)
                    a_i = jnp.where(lane_c == r0 + s, col, a_i)
                a_rows.append(a_i)
            a = a_rows[0] if n_sub == 1 else jnp.concatenate(a_rows, axis=0)
            v16 = v.astype(BF16)
            o = o + _dot(a.astype(BF16), v16)

            g_last = gc[chunk - 1:chunk]
            kd = (kh * jnp.exp(g_last - gc)).astype(BF16)
            st_ref[h] = jnp.exp(g_last) * st + _dot_tn(v16, kd)

            o_ref[rows, cols] = _rms(o, g_ref[...]) * (xg * _sigmoid(xg))

    @pl.when(t_id == pl.num_programs(1) - 1)
    def _():
        for h in range(B_HEADS):
            sfin_ref[h] = st_ref[h].T


def _hgrn(z, lb_logits, norm_g, s0, layer, batch, seq, tt, chunk, sub):
    zv = z.reshape(batch, seq, N_SPLITS * A_WIDTH)
    zspec = lambda col: pl.BlockSpec((None, tt, B_WIDTH), lambda b, t: (b, t, col))
    sspec = pl.BlockSpec((None, B_HEADS, B_DIM, B_DIM), lambda b, t: (b, 0, 0, 0))
    depth = lb_logits.shape[0]
    o, sfin = pl.pallas_call(
        functools.partial(_hgrn_kernel, layer=layer, chunk=chunk, sub=sub),
        grid=(batch, seq // tt),
        in_specs=[zspec(3), zspec(4), zspec(5), zspec(6),
                  pl.BlockSpec((depth, B_WIDTH), lambda b, t: (0, 0)),
                  pl.BlockSpec((None, 1, B_DIM), lambda b, t: (layer, 0, 0)),
                  sspec],
        out_specs=[pl.BlockSpec((None, tt, B_WIDTH), lambda b, t: (b, t, 0)), sspec],
        out_shape=[jax.ShapeDtypeStruct((batch, seq, B_WIDTH), F32),
                   jax.ShapeDtypeStruct((batch, B_HEADS, B_DIM, B_DIM), F32)],
        scratch_shapes=[pltpu.VMEM((B_HEADS, B_DIM, B_DIM), F32)],
        compiler_params=_cparams(2),
        name="hgrn",
    )(zv, zv, zv, zv, lb_logits, norm_g, s0)
    return o.reshape(batch * seq, B_WIDTH), sfin


def _mix_out_kernel(att_ref, ob_ref, h_ref, attn_g_ref, w_ref, post_g_ref, o_ref):
    oa = _rms(att_ref[...], attn_g_ref[...]).astype(BF16)
    m = _dot(oa, w_ref[:A_WIDTH, :]) + _dot(ob_ref[...].astype(BF16), w_ref[A_WIDTH:, :])
    o_ref[...] = h_ref[...] + _rms(m, post_g_ref[...])


def _mix_out(att, ob, h, attn_g, w_out, post_g, layer, tm):
    m, d = h.shape
    half = pl.BlockSpec((tm, A_WIDTH), lambda i: (i, 0))
    return pl.pallas_call(
        _mix_out_kernel,
        grid=(m // tm,),
        in_specs=[
            half, half,
            pl.BlockSpec((tm, d), lambda i: (i, 0)),
            pl.BlockSpec((None, 1, A_WIDTH), lambda i: (layer, 0, 0)),
            pl.BlockSpec((None, A_WIDTH + B_WIDTH, d), lambda i: (layer, 0, 0)),
            pl.BlockSpec((None, 1, d), lambda i: (layer, 0, 0)),
        ],
        out_specs=pl.BlockSpec((tm, d), lambda i: (i, 0)),
        out_shape=jax.ShapeDtypeStruct((m, d), F32),
        compiler_params=_cparams(1),
        name="mix_out",
    )(att, ob, h, attn_g, w_out, post_g)


def kernel(x_prompt, x_sample, cache_attn_k, cache_attn_v, state_hgrn, ff1_pre_g, ff1_w_gate, ff1_w_up, ff1_w_down, ff1_post_g, mix_pre_g, w_in, attn_norm_g, hgrn_lb_logits, hgrn_norm_g, w_out, mix_post_g, ff2_pre_g, ff2_w_gate, ff2_w_up, ff2_w_down, ff2_post_g):
    batch, seq, d = x_prompt.shape
    dec_batch, dec_seq, _ = x_sample.shape
    depth = w_in.shape[0]
    keep = min(DILATIONS[-1] * WINDOW_STEPS, seq)

    row = lambda g: g.reshape(depth, 1, g.shape[-1])
    bf = lambda w: w.astype(BF16)
    ff1 = (row(ff1_pre_g), bf(ff1_w_gate), bf(ff1_w_up), bf(ff1_w_down), row(ff1_post_g))
    ff2 = (row(ff2_pre_g), bf(ff2_w_gate), bf(ff2_w_up), bf(ff2_w_down), row(ff2_post_g))
    mix_pre_g, attn_norm_g, hgrn_norm_g, mix_post_g = map(row, (mix_pre_g, attn_norm_g, hgrn_norm_g, mix_post_g))
    w_in16, w_out16 = bf(w_in), bf(w_out)
    zero_state = jnp.zeros((batch, B_HEADS, B_DIM, B_DIM), F32)

    hp = x_prompt.reshape(batch * seq, d)
    hs = x_sample.reshape(dec_batch * dec_seq, d)
    tm_p, tm_s, tf = 1024, dec_batch * dec_seq, 256
    zps = []
    outs = {k: [] for k in ("sp", "ks", "vs", "ss")}
    for l in range(depth):
        hp = _ffn(hp, *ff1, l, tm_p, tf)
        zp = _mix_in(hp, mix_pre_g, w_in16, l, tm_p)
        att = _attn_prompt(zp, batch, seq)
        ob, sp = _hgrn(zp, hgrn_lb_logits, hgrn_norm_g, zero_state, l, batch, seq,
                       2 * HGRN_CHUNK, HGRN_CHUNK, HGRN_SUB)
        hp = _mix_out(att, ob, hp, attn_norm_g, w_out16, mix_post_g, l, 512)
        hp = _ffn(hp, *ff2, l, tm_p, tf)
        zps.append(zp)
        outs["sp"].append(sp)

        hs = _ffn(hs, *ff1, l, tm_s, tf)
        zs = _mix_in(hs, mix_pre_g, w_in16, l, tm_s)
        att_s = _attn_sample(zs, cache_attn_k, cache_attn_v, l, dec_batch, dec_seq)
        ob_s, ss = _hgrn(zs, hgrn_lb_logits, hgrn_norm_g, state_hgrn[l], l, dec_batch, dec_seq,
                         dec_seq, dec_seq, dec_seq)
        hs = _mix_out(att_s, ob_s, hs, attn_norm_g, w_out16, mix_post_g, l, tm_s)
        hs = _ffn(hs, *ff2, l, tm_s, tf)
        zs3 = zs.reshape(dec_batch, dec_seq, N_SPLITS, A_HEADS, A_HEAD_DIM)
        outs["ks"].append(zs3[:, :, 1])
        outs["vs"].append(zs3[:, :, 2])
        outs["ss"].append(ss)

    kp, vp = _kv_format(zps, batch, seq, keep, 512)
    stack = lambda k: jnp.stack(outs[k])
    return (hp.reshape(batch, seq, d), hs.reshape(dec_batch, dec_seq, d),
            kp, vp, stack("sp"), stack("ks"), stack("vs"), stack("ss"))
```

```python
import functools
import math

import jax
import jax.numpy as jnp
from jax import lax
from jax.experimental import pallas as pl
from jax.experimental.pallas import tpu as pltpu

F32 = jnp.float32
BF16 = jnp.bfloat16

NORM_EPS = 1e-6
LOG2_E = math.log2(math.e)
A_HEADS = 8
A_HEAD_DIM = 64
A_WIDTH = A_HEADS * A_HEAD_DIM
A_PAIR = 2 * A_HEAD_DIM
B_HEADS = 4
B_DIM = 128
B_WIDTH = B_HEADS * B_DIM
DILATIONS = (1, 4, 16)
WINDOW_STEPS = 128
N_SPLITS = 7
HGRN_CHUNK = 64
HGRN_SUB = 8

V7X_VMEM_LIMIT_BYTES = 56 * 1024 * 1024


def _cparams(n_grid_axes, parallel_first=True):
    first = "parallel" if parallel_first else "arbitrary"
    sem = (first,) + ("arbitrary",) * (n_grid_axes - 1)
    return pltpu.CompilerParams(dimension_semantics=sem,
                                vmem_limit_bytes=V7X_VMEM_LIMIT_BYTES)


def _rms(x, g):
    return x * lax.rsqrt(jnp.mean(x * x, axis=-1, keepdims=True) + NORM_EPS) * g


def _sigmoid(x):
    return 1.0 / (1.0 + jnp.exp(-x))


def _dot(a, b):
    return jnp.dot(a, b, preferred_element_type=F32)


def _dot_nt(a, b):
    return lax.dot_general(a, b, (((1,), (1,)), ((), ())), preferred_element_type=F32)


def _dot_tn(a, b):
    return lax.dot_general(a, b, (((0,), (0,)), ((), ())), preferred_element_type=F32)


def _ffn_kernel(x_ref, pre_g_ref, wg_ref, wu_ref, wd_ref, post_g_ref, o_ref, xn_ref, acc_ref, *, tf):
    xn_ref[...] = _rms(x_ref[...], pre_g_ref[...]).astype(BF16)
    ff = wg_ref.shape[-1]
    for c in range(ff // tf):
        cols = slice(c * tf, (c + 1) * tf)
        xn = xn_ref[...]
        g = _dot(xn, wg_ref[:, cols])
        u = _dot(xn, wu_ref[:, cols])
        a = (g * _sigmoid(g) * u).astype(BF16)
        part = _dot(a, wd_ref[cols, :])
        if c == 0:
            acc_ref[...] = part
        else:
            acc_ref[...] += part
    o_ref[...] = x_ref[...] + 0.5 * _rms(acc_ref[...], post_g_ref[...])


def _resident(block_shape, index_map):
    return pl.BlockSpec(block_shape, index_map, pipeline_mode=pl.Buffered(1))


def _ffn(x, pre_g, wg, wu, wd, post_g, layer, tm, tf):
    m, d = x.shape
    ff = wg.shape[-1]
    return pl.pallas_call(
        functools.partial(_ffn_kernel, tf=tf),
        grid=(m // tm,),
        in_specs=[
            pl.BlockSpec((tm, d), lambda i: (i, 0)),
            _resident((None, 1, d), lambda i: (layer, 0, 0)),
            _resident((None, d, ff), lambda i: (layer, 0, 0)),
            _resident((None, d, ff), lambda i: (layer, 0, 0)),
            _resident((None, ff, d), lambda i: (layer, 0, 0)),
            _resident((None, 1, d), lambda i: (layer, 0, 0)),
        ],
        out_specs=pl.BlockSpec((tm, d), lambda i: (i, 0)),
        out_shape=jax.ShapeDtypeStruct((m, d), F32),
        scratch_shapes=[pltpu.VMEM((tm, d), BF16), pltpu.VMEM((tm, d), F32)],
        compiler_params=_cparams(1),
        name="ffn",
    )(x, pre_g, wg, wu, wd, post_g)


def _mix_in_kernel(h_ref, g_ref, w_ref, z_ref, hn_ref):
    hn_ref[...] = _rms(h_ref[...], g_ref[...]).astype(BF16)
    for c in range(N_SPLITS):
        cols = slice(c * A_WIDTH, (c + 1) * A_WIDTH)
        z_ref[:, cols] = _dot(hn_ref[...], w_ref[:, cols])


def _mix_in(h, g, w_in, layer, tm):
    m, d = h.shape
    n = w_in.shape[-1]
    return pl.pallas_call(
        _mix_in_kernel,
        grid=(m // tm,),
        in_specs=[
            pl.BlockSpec((tm, d), lambda i: (i, 0)),
            _resident((None, 1, d), lambda i: (layer, 0, 0)),
            _resident((None, d, n), lambda i: (layer, 0, 0)),
        ],
        out_specs=pl.BlockSpec((tm, n), lambda i: (i, 0)),
        out_shape=jax.ShapeDtypeStruct((m, n), F32),
        scratch_shapes=[pltpu.VMEM((tm, d), BF16)],
        compiler_params=_cparams(1),
        name="mix_in",
    )(h, g, w_in)


def _attn_block(q, k_prev, k_own, v_prev, v_own, bias, first_head):
    w = q.shape[0]
    q = q * (1.0 / math.sqrt(A_HEAD_DIM))
    q2 = jnp.concatenate([jnp.where(first_head, q, 0.0), jnp.where(first_head, 0.0, q)], axis=0)
    kcat = jnp.concatenate([k_prev, k_own], axis=0).astype(BF16)
    vcat = jnp.concatenate([v_prev, v_own], axis=0).astype(BF16)
    s = _dot_nt(q2.astype(BF16), kcat) + jnp.concatenate([bias, bias], axis=0)
    m_row = jnp.max(s, axis=-1, keepdims=True)
    p = jnp.exp(s - m_row)
    l_row = jnp.sum(p, axis=-1, keepdims=True)
    pv = _dot(p.astype(BF16), vcat)
    pick = lambda x: jnp.where(first_head, x[:w], x[w:])
    return (pick(jnp.broadcast_to(m_row, (2 * w, A_PAIR))),
            pick(jnp.broadcast_to(l_row, (2 * w, A_PAIR))), pick(pv))


def _attn_merge(m_o, l_o, acc_o, m_b, l_b, acc_b):
    m_n = jnp.maximum(m_o, m_b)
    a_o = jnp.exp(m_o - m_n)
    a_b = jnp.exp(m_b - m_n)
    return m_n, a_o * l_o + a_b * l_b, a_o * acc_o + a_b * acc_b


def _attn_prompt_kernel(q_ref, k_ref, v_ref, o_ref, m_ref, l_ref, acc_ref, q4_ref, k4_ref, v4_ref,
                        m4_ref, l4_ref, acc4_ref, bias_ref, *, unroll):
    w = WINDOW_STEPS
    seq = q_ref.shape[0]
    d1, d2 = DILATIONS[1], DILATIONS[2]
    ratio = d2 // d1
    per_res = seq // d1
    nb1 = per_res // w
    nb2 = seq // (d2 * w)
    first_head = lax.broadcasted_iota(jnp.int32, (w, A_PAIR), 1) < A_HEAD_DIM
    qi = lax.broadcasted_iota(jnp.int32, (w, 2 * w), 0)
    ki = lax.broadcasted_iota(jnp.int32, (w, 2 * w), 1)
    dist = w + qi - ki
    band = (dist >= 0) & (dist <= w)
    bias_ref[0] = jnp.where(band & (ki >= w), 0.0, -jnp.inf)
    bias_ref[1] = jnp.where(band, 0.0, -jnp.inf)

    def pattern0(c, carry):
        rows = pl.ds(pl.multiple_of(c * w, w), w)
        prev = pl.ds(pl.multiple_of(jnp.maximum(c - 1, 0) * w, w), w)
        m_b, l_b, acc_b = _attn_block(q_ref[rows, :], k_ref[prev, :], k_ref[rows, :], v_ref[prev, :],
                                      v_ref[rows, :], bias_ref[jnp.minimum(c, 1)], first_head)
        m_ref[rows, :] = m_b
        l_ref[rows, :] = l_b
        acc_ref[rows, :] = acc_b
        return carry

    lax.fori_loop(0, seq // w, pattern0, 0, unroll=unroll)

    for r in range(d1):
        dst = pl.ds(r * per_res, per_res)
        src = pl.ds(r, per_res, stride=d1)
        for nat, grouped in ((q_ref, q4_ref), (k_ref, k4_ref), (v_ref, v4_ref),
                             (m_ref, m4_ref), (l_ref, l4_ref), (acc_ref, acc4_ref)):
            grouped[dst, :] = nat[src, :]

    def pattern1(idx, carry):
        has_prev = jnp.minimum(idx & (nb1 - 1), 1)
        rows = pl.ds(pl.multiple_of(idx * w, w), w)
        prev = pl.ds(pl.multiple_of((idx - has_prev) * w, w), w)
        blk = _attn_block(q4_ref[rows, :], k4_ref[prev, :], k4_ref[rows, :], v4_ref[prev, :],
                          v4_ref[rows, :], bias_ref[has_prev], first_head)
        m_n, l_n, acc_n = _attn_merge(m4_ref[rows, :], l4_ref[rows, :], acc4_ref[rows, :], *blk)
        m4_ref[rows, :] = m_n
        l4_ref[rows, :] = l_n
        acc4_ref[rows, :] = acc_n
        return carry

    assert nb1 & (nb1 - 1) == 0 and nb2 & (nb2 - 1) == 0 and ratio & (ratio - 1) == 0
    lax.fori_loop(0, seq // w, pattern1, 0, unroll=unroll)

    def pattern2(idx, carry):
        a = idx & (ratio - 1)
        c = (idx >> (ratio.bit_length() - 1)) & (nb2 - 1)
        r = idx >> ((ratio * nb2).bit_length() - 1)
        has_prev = jnp.minimum(c, 1)
        start = r * per_res + c * (w * ratio) + a
        rows = pl.ds(start, w, stride=ratio)
        prev = pl.ds(start - has_prev * (w * ratio), w, stride=ratio)
        blk = _attn_block(q4_ref[rows, :], k4_ref[prev, :], k4_ref[rows, :], v4_ref[prev, :],
                          v4_ref[rows, :], bias_ref[has_prev], first_head)
        _, l_n, acc_n = _attn_merge(m4_ref[rows, :], l4_ref[rows, :], acc4_ref[rows, :], *blk)
        acc_ref[rows, :] = acc_n / l_n
        return carry

    lax.fori_loop(0, seq // w, pattern2, 0, unroll=unroll)

    for r in range(d1):
        o_ref[pl.ds(r, per_res, stride=d1), :] = acc_ref[pl.ds(r * per_res, per_res), :]


def _attn_prompt(z, batch, seq, unroll=8):
    assert DILATIONS[0] == 1 and len(DILATIONS) == 3 and DILATIONS[2] % DILATIONS[1] == 0
    assert seq % (DILATIONS[-1] * WINDOW_STEPS) == 0
    zv = z.reshape(batch, seq, N_SPLITS * A_WIDTH)
    n_pairs = A_WIDTH // A_PAIR
    spec = lambda split: pl.BlockSpec((None, seq, A_PAIR), lambda b, p: (b, 0, split * n_pairs + p))
    o = pl.pallas_call(
        functools.partial(_attn_prompt_kernel, unroll=unroll),
        grid=(batch, n_pairs),
        in_specs=[spec(0), spec(1), spec(2)],
        out_specs=pl.BlockSpec((None, seq, A_PAIR), lambda b, p: (b, 0, p)),
        out_shape=jax.ShapeDtypeStruct((batch, seq, A_WIDTH), F32),
        scratch_shapes=[pltpu.VMEM((seq, A_PAIR), F32)] * 9
        + [pltpu.VMEM((2, WINDOW_STEPS, 2 * WINDOW_STEPS), F32)],
        compiler_params=_cparams(2),
        name="attn_prompt",
    )(zv, zv, zv)
    return o.reshape(batch * seq, A_WIDTH)


def _pattern_count(delta):
    cnt = jnp.zeros(delta.shape, F32)
    for dil in DILATIONS:
        hit = (delta >= 0) & ((delta & (dil - 1)) == 0) & (delta <= dil * WINDOW_STEPS)
        cnt = cnt + hit.astype(F32)
    return cnt


def _attn_sample_kernel(q_ref, kn_ref, vn_ref, kt_ref, vt_ref, o_ref):
    t = q_ref.shape[0]
    hq = A_HEADS * t
    n_cache = kt_ref.shape[-1]
    assert t & (t - 1) == 0
    t_shift = t.bit_length() - 1

    def weights(n_cols, first_pos):
        query = lax.broadcasted_iota(jnp.int32, (hq, n_cols), 0) & (t - 1)
        key = first_pos + lax.broadcasted_iota(jnp.int32, (hq, n_cols), 1)
        return _pattern_count(n_cache + query - key)

    w_cache = weights(n_cache, 0)
    w_new = weights(t, n_cache)

    row_head = lax.broadcasted_iota(jnp.int32, (hq, A_WIDTH), 0) >> t_shift
    lane_head = lax.broadcasted_iota(jnp.int32, (hq, A_WIDTH), 1) >> (A_HEAD_DIM.bit_length() - 1)
    own_head = row_head == lane_head
    q = q_ref[...] * (1.0 / math.sqrt(A_HEAD_DIM))
    q_bd = jnp.where(own_head, jnp.concatenate([q] * A_HEADS, axis=0), 0.0).astype(BF16)

    kt = kt_ref[...].reshape(A_WIDTH, n_cache).astype(BF16)
    vt = vt_ref[...].reshape(A_WIDTH, n_cache).astype(BF16)
    s_c = jnp.where(w_cache > 0, _dot(q_bd, kt), -jnp.inf)
    s_n = jnp.where(w_new > 0, _dot_nt(q_bd, kn_ref[...].astype(BF16)), -jnp.inf)
    m = jnp.maximum(jnp.max(s_c, axis=-1, keepdims=True), jnp.max(s_n, axis=-1, keepdims=True))
    p_c = w_cache * jnp.exp(s_c - m)
    p_n = w_new * jnp.exp(s_n - m)
    l = jnp.sum(p_c, axis=-1, keepdims=True) + jnp.sum(p_n, axis=-1, keepdims=True)
    r = (_dot_nt(p_c.astype(BF16), vt) + _dot(p_n.astype(BF16), vn_ref[...].astype(BF16))) / l
    r = jnp.where(own_head, r, 0.0)
    out = r[0:t]
    for h in range(1, A_HEADS):
        out = out + r[h * t:(h + 1) * t]
    o_ref[...] = out


def _attn_sample(z, cache_k, cache_v, layer, batch, t):
    n_cache = cache_k.shape[2]
    zv = z.reshape(batch, t, N_SPLITS * A_WIDTH)
    ckt = jnp.transpose(cache_k, (0, 1, 3, 4, 2))
    cvt = jnp.transpose(cache_v, (0, 1, 3, 4, 2))
    zspec = lambda col: pl.BlockSpec((None, t, A_WIDTH), lambda b: (b, 0, col))
    cspec = pl.BlockSpec((None, None, A_HEADS, A_HEAD_DIM, n_cache), lambda b: (layer, b, 0, 0, 0))
    o = pl.pallas_call(
        _attn_sample_kernel,
        grid=(batch,),
        in_specs=[zspec(0), zspec(1), zspec(2), cspec, cspec],
        out_specs=pl.BlockSpec((None, t, A_WIDTH), lambda b: (b, 0, 0)),
        out_shape=jax.ShapeDtypeStruct((batch, t, A_WIDTH), F32),
        compiler_params=_cparams(1),
        name="attn_sample",
    )(zv, zv, zv, ckt, cvt)
    return o.reshape(batch * t, A_WIDTH)


def _kv_format_kernel(*refs, depth):
    k_refs, v_refs = refs[:depth], refs[depth:2 * depth]
    ko_ref, vo_ref = refs[2 * depth:]
    tm = ko_ref.shape[-1]
    for lyr in range(depth):
        @pl.when(pl.program_id(0) == lyr)
        def _(lyr=lyr):
            for src, dst in ((k_refs[lyr], ko_ref), (v_refs[lyr], vo_ref)):
                dst[...] = src[...].T.reshape(A_HEADS, A_HEAD_DIM, tm)


def _kv_format(zs, batch, seq, keep, tm):
    depth = len(zs)
    zvs = [z.reshape(batch, seq, N_SPLITS * A_WIDTH) for z in zs]
    t0, nt = (seq - keep) // tm, keep // tm

    def in_spec(lyr, split):
        def index(l, b, t):
            before, after = l < lyr, l > lyr
            bb = jnp.where(before, 0, jnp.where(after, batch - 1, b))
            tt = jnp.where(before, 0, jnp.where(after, nt - 1, t))
            return (bb, t0 + tt, split)
        return pl.BlockSpec((None, tm, A_WIDTH), index)

    out_spec = pl.BlockSpec((None, None, A_HEADS, A_HEAD_DIM, tm), lambda l, b, t: (l, b, 0, 0, t))
    shape = jax.ShapeDtypeStruct((depth, batch, A_HEADS, A_HEAD_DIM, keep), F32)
    kt, vt = pl.pallas_call(
        functools.partial(_kv_format_kernel, depth=depth),
        grid=(depth, batch, nt),
        in_specs=[in_spec(l, 1) for l in range(depth)] + [in_spec(l, 2) for l in range(depth)],
        out_specs=[out_spec, out_spec],
        out_shape=[shape, shape],
        compiler_params=_cparams(3, parallel_first=False),
        name="kv_format",
    )(*zvs, *zvs)
    return jnp.transpose(kt, (0, 1, 4, 2, 3)), jnp.transpose(vt, (0, 1, 4, 2, 3))


def _hgrn_kernel(xq_ref, xf_ref, xi_ref, xg_ref, lbl_ref, g_ref, s0_ref, o_ref, sfin_ref, st_ref, g2_scr, k_scr,
                 *, layer, chunk, sub):
    t_id = pl.program_id(1)

    @pl.when(t_id == 0)
    def _():
        for h in range(B_HEADS):
            st_ref[h] = s0_ref[h].T

    logits = lbl_ref[...]
    e = jnp.exp(logits - jnp.max(logits, axis=0, keepdims=True))
    soft = e / jnp.sum(e, axis=0, keepdims=True)
    lb_all = soft[0:1] - soft[0:1]
    for l in range(1, layer + 1):
        lb_all = lb_all + soft[l:l + 1]

    n_chunks = xq_ref.shape[0] // chunk
    n_sub = chunk // sub
    t_idx = lax.broadcasted_iota(jnp.int32, (chunk, chunk), 0)
    s_idx = lax.broadcasted_iota(jnp.int32, (chunk, chunk), 1)
    tri = (t_idx >= s_idx).astype(BF16)
    row_c = lax.broadcasted_iota(jnp.int32, (chunk, B_DIM), 0)
    sub_t = lax.broadcasted_iota(jnp.int32, (sub, chunk), 0)
    sub_c = lax.broadcasted_iota(jnp.int32, (sub, chunk), 1)
    diag_code = [jnp.where((sub_c >= i * sub) & (sub_c - i * sub <= sub_t), sub_c - i * sub, -1)
                 for i in range(n_sub)]
    halves = []
    half = chunk // 2
    while half >= sub:
        halves.append(half)
        half //= 2

    for ci in range(n_chunks):
        rows = slice(ci * chunk, (ci + 1) * chunk)
        for h in range(B_HEADS):
            cols = slice(h * B_DIM, (h + 1) * B_DIM)
            xq = xq_ref[rows, cols]
            xg = xg_ref[rows, cols]
            v = xi_ref[rows, cols]
            lb = lb_all[:, cols]
            qh = xq * _sigmoid(xq)
            f = lb + (1.0 - lb) * _sigmoid(xf_ref[rows, cols])
            logf = jnp.log(f)
            kh = 1.0 - f

            hi = logf.astype(BF16)
            r1 = logf - hi.astype(F32)
            mid = r1.astype(BF16)
            lo = (r1 - mid.astype(F32)).astype(BF16)
            g2 = (_dot(tri, hi) + _dot(tri, mid) + _dot(tri, lo)) * LOG2_E

            st = st_ref[h]
            o = _dot_nt((qh * jnp.exp2(g2)).astype(BF16), st.astype(BF16))

            slot = ci * B_HEADS + h
            g2_scr[slot] = g2
            k_scr[slot] = kh
            a_rows = []
            for i in range(n_sub):
                r0 = i * sub
                gi = g2[r0:r0 + sub]
                qi = qh[r0:r0 + sub]
                a_i = jnp.zeros((sub, chunk), F32)
                for s in range(sub):
                    dec = jnp.exp2(gi - g2_scr[slot, pl.ds(r0 + s, 1), :])
                    col = jnp.sum(qi * k_scr[slot, pl.ds(r0 + s, 1), :] * dec, axis=-1, keepdims=True)
                    a_i = jnp.where(diag_code[i] == s, col, a_i)
                a_rows.append(a_i)
            a = a_rows[0] if n_sub == 1 else jnp.concatenate(a_rows, axis=0)
            for half in halves:
                shift = half.bit_length() - 1
                bounds = [jnp.broadcast_to(g2[r0 + half - 1:r0 + half], (2 * half, B_DIM))
                          for r0 in range(0, chunk, 2 * half)]
                bnd = bounds[0] if len(bounds) == 1 else jnp.concatenate(bounds, axis=0)
                e_half = jnp.exp2(-jnp.abs(g2 - bnd))
                upper = ((row_c >> shift) & 1) == 1
                qo = jnp.where(upper, qh * e_half, 0.0).astype(BF16)
                kk = jnp.where(upper, 0.0, kh * e_half).astype(BF16)
                a_half = _dot_nt(qo, kk)
                if 2 * half < chunk:
                    a_half = jnp.where((t_idx >> (shift + 1)) == (s_idx >> (shift + 1)), a_half, 0.0)
                a = a + a_half
            v16 = v.astype(BF16)
            o = o + _dot(a.astype(BF16), v16)

            g_last = g2[chunk - 1:chunk]
            kd = (kh * jnp.exp2(g_last - g2)).astype(BF16)
            st_ref[h] = jnp.exp2(g_last) * st + _dot_tn(v16, kd)

            o_ref[rows, cols] = _rms(o, g_ref[...]) * (xg * _sigmoid(xg))

    @pl.when(t_id == pl.num_programs(1) - 1)
    def _():
        for h in range(B_HEADS):
            sfin_ref[h] = st_ref[h].T


def _hgrn(z, lb_logits, norm_g, s0, layer, batch, seq, tt, chunk, sub):
    zv = z.reshape(batch, seq, N_SPLITS * A_WIDTH)
    zspec = lambda col: pl.BlockSpec((None, tt, B_WIDTH), lambda b, t: (b, t, col))
    sspec = pl.BlockSpec((None, B_HEADS, B_DIM, B_DIM), lambda b, t: (b, 0, 0, 0))
    depth = lb_logits.shape[0]
    o, sfin = pl.pallas_call(
        functools.partial(_hgrn_kernel, layer=layer, chunk=chunk, sub=sub),
        grid=(batch, seq // tt),
        in_specs=[zspec(3), zspec(4), zspec(5), zspec(6),
                  pl.BlockSpec((depth, B_WIDTH), lambda b, t: (0, 0)),
                  pl.BlockSpec((None, 1, B_DIM), lambda b, t: (layer, 0, 0)),
                  sspec],
        out_specs=[pl.BlockSpec((None, tt, B_WIDTH), lambda b, t: (b, t, 0)), sspec],
        out_shape=[jax.ShapeDtypeStruct((batch, seq, B_WIDTH), F32),
                   jax.ShapeDtypeStruct((batch, B_HEADS, B_DIM, B_DIM), F32)],
        scratch_shapes=[pltpu.VMEM((B_HEADS, B_DIM, B_DIM), F32)]
        + [pltpu.VMEM((tt // chunk * B_HEADS, chunk, B_DIM), F32)] * 2,
        compiler_params=_cparams(2),
        name="hgrn",
    )(zv, zv, zv, zv, lb_logits, norm_g, s0)
    return o.reshape(batch * seq, B_WIDTH), sfin


def _mix_out_kernel(att_ref, ob_ref, h_ref, attn_g_ref, w_ref, post_g_ref, o_ref):
    oa = _rms(att_ref[...], attn_g_ref[...]).astype(BF16)
    m = _dot(oa, w_ref[:A_WIDTH, :]) + _dot(ob_ref[...].astype(BF16), w_ref[A_WIDTH:, :])
    o_ref[...] = h_ref[...] + _rms(m, post_g_ref[...])


def _mix_out(att, ob, h, attn_g, w_out, post_g, layer, tm):
    m, d = h.shape
    half = pl.BlockSpec((tm, A_WIDTH), lambda i: (i, 0))
    return pl.pallas_call(
        _mix_out_kernel,
        grid=(m // tm,),
        in_specs=[
            half, half,
            pl.BlockSpec((tm, d), lambda i: (i, 0)),
            pl.BlockSpec((None, 1, A_WIDTH), lambda i: (layer, 0, 0)),
            pl.BlockSpec((None, A_WIDTH + B_WIDTH, d), lambda i: (layer, 0, 0)),
            pl.BlockSpec((None, 1, d), lambda i: (layer, 0, 0)),
        ],
        out_specs=pl.BlockSpec((tm, d), lambda i: (i, 0)),
        out_shape=jax.ShapeDtypeStruct((m, d), F32),
        compiler_params=_cparams(1),
        name="mix_out",
    )(att, ob, h, attn_g, w_out, post_g)


def kernel(x_prompt, x_sample, cache_attn_k, cache_attn_v, state_hgrn, ff1_pre_g, ff1_w_gate, ff1_w_up, ff1_w_down, ff1_post_g, mix_pre_g, w_in, attn_norm_g, hgrn_lb_logits, hgrn_norm_g, w_out, mix_post_g, ff2_pre_g, ff2_w_gate, ff2_w_up, ff2_w_down, ff2_post_g):
    batch, seq, d = x_prompt.shape
    dec_batch, dec_seq, _ = x_sample.shape
    depth = w_in.shape[0]
    keep = min(DILATIONS[-1] * WINDOW_STEPS, seq)

    row = lambda g: g.reshape(depth, 1, g.shape[-1])
    bf = lambda w: w.astype(BF16)
    ff1 = (row(ff1_pre_g), bf(ff1_w_gate), bf(ff1_w_up), bf(ff1_w_down), row(ff1_post_g))
    ff2 = (row(ff2_pre_g), bf(ff2_w_gate), bf(ff2_w_up), bf(ff2_w_down), row(ff2_post_g))
    mix_pre_g, attn_norm_g, hgrn_norm_g, mix_post_g = map(row, (mix_pre_g, attn_norm_g, hgrn_norm_g, mix_post_g))
    w_in16, w_out16 = bf(w_in), bf(w_out)
    zero_state = jnp.zeros((batch, B_HEADS, B_DIM, B_DIM), F32)

    hp = x_prompt.reshape(batch * seq, d)
    hs = x_sample.reshape(dec_batch * dec_seq, d)
    tm_p, tm_s, tf = 1024, dec_batch * dec_seq, 256
    zps = []
    outs = {k: [] for k in ("sp", "ks", "vs", "ss")}
    for l in range(depth):
        hp = _ffn(hp, *ff1, l, tm_p, tf)
        zp = _mix_in(hp, mix_pre_g, w_in16, l, 512)
        att = _attn_prompt(zp, batch, seq)
        ob, sp = _hgrn(zp, hgrn_lb_logits, hgrn_norm_g, zero_state, l, batch, seq,
                       4 * HGRN_CHUNK, HGRN_CHUNK, HGRN_SUB)
        hp = _mix_out(att, ob, hp, attn_norm_g, w_out16, mix_post_g, l, 512)
        hp = _ffn(hp, *ff2, l, tm_p, tf)
        zps.append(zp)
        outs["sp"].append(sp)

        hs = _ffn(hs, *ff1, l, tm_s, tf)
        zs = _mix_in(hs, mix_pre_g, w_in16, l, tm_s)
        att_s = _attn_sample(zs, cache_attn_k, cache_attn_v, l, dec_batch, dec_seq)
        ob_s, ss = _hgrn(zs, hgrn_lb_logits, hgrn_norm_g, state_hgrn[l], l, dec_batch, dec_seq,
                         dec_seq, dec_seq, dec_seq)
        hs = _mix_out(att_s, ob_s, hs, attn_norm_g, w_out16, mix_post_g, l, tm_s)
        hs = _ffn(hs, *ff2, l, tm_s, tf)
        zs3 = zs.reshape(dec_batch, dec_seq, N_SPLITS, A_HEADS, A_HEAD_DIM)
        outs["ks"].append(zs3[:, :, 1])
        outs["vs"].append(zs3[:, :, 2])
        outs["ss"].append(ss)

    kp, vp = _kv_format(zps, batch, seq, keep, 512)
    stack = lambda k: jnp.stack(outs[k])
    return (hp.reshape(batch, seq, d), hs.reshape(dec_batch, dec_seq, d),
            kp, vp, stack("sp"), stack("ks"), stack("vs"), stack("ss"))
```

```python
import functools
import math

import jax
import jax.numpy as jnp
from jax import lax
from jax.experimental import pallas as pl
from jax.experimental.pallas import tpu as pltpu

F32 = jnp.float32
BF16 = jnp.bfloat16

NORM_EPS = 1e-6
LOG2_E = math.log2(math.e)
A_HEADS = 8
A_HEAD_DIM = 64
A_WIDTH = A_HEADS * A_HEAD_DIM
A_PAIR = 2 * A_HEAD_DIM
B_HEADS = 4
B_DIM = 128
B_WIDTH = B_HEADS * B_DIM
DILATIONS = (1, 4, 16)
WINDOW_STEPS = 128
N_SPLITS = 7
HGRN_CHUNK = 64
HGRN_SUB = 8

V7X_VMEM_LIMIT_BYTES = 56 * 1024 * 1024


def _cparams(n_grid_axes, parallel_first=True):
    first = "parallel" if parallel_first else "arbitrary"
    sem = (first,) + ("arbitrary",) * (n_grid_axes - 1)
    return pltpu.CompilerParams(dimension_semantics=sem,
                                vmem_limit_bytes=V7X_VMEM_LIMIT_BYTES)


def _rms(x, g):
    return x * lax.rsqrt(jnp.mean(x * x, axis=-1, keepdims=True) + NORM_EPS) * g


def _sigmoid(x):
    return 1.0 / (1.0 + jnp.exp(-x))


def _dot(a, b):
    return jnp.dot(a, b, preferred_element_type=F32)


def _dot_nt(a, b):
    return lax.dot_general(a, b, (((1,), (1,)), ((), ())), preferred_element_type=F32)


def _dot_tn(a, b):
    return lax.dot_general(a, b, (((0,), (0,)), ((), ())), preferred_element_type=F32)


def _ffn_kernel(*refs, tf, mixer):
    if mixer:
        att_ref, ob_ref, attn_g_ref, wo_ref, mix_post_g_ref = refs[:5]
        refs = refs[5:]
    x_ref, pre_g_ref, wg_ref, wu_ref, wd_ref, post_g_ref, o_ref, xn_ref, acc_ref = refs
    if mixer:
        oa = _rms(att_ref[...], attn_g_ref[...]).astype(BF16)
        m = _dot(oa, wo_ref[:A_WIDTH, :]) + _dot(ob_ref[...].astype(BF16), wo_ref[A_WIDTH:, :])
        o_ref[...] = x_ref[...] + _rms(m, mix_post_g_ref[...])
        res_ref = o_ref
    else:
        res_ref = x_ref
    xn_ref[...] = _rms(res_ref[...], pre_g_ref[...]).astype(BF16)
    ff = wg_ref.shape[-1]
    for c in range(ff // tf):
        cols = slice(c * tf, (c + 1) * tf)
        xn = xn_ref[...]
        g = _dot(xn, wg_ref[:, cols])
        u = _dot(xn, wu_ref[:, cols])
        a = (g * _sigmoid(g) * u).astype(BF16)
        part = _dot(a, wd_ref[cols, :])
        if c == 0:
            acc_ref[...] = part
        else:
            acc_ref[...] += part
    o_ref[...] = res_ref[...] + 0.5 * _rms(acc_ref[...], post_g_ref[...])


def _resident(block_shape, index_map):
    return pl.BlockSpec(block_shape, index_map, pipeline_mode=pl.Buffered(1))


def _ffn(x, pre_g, wg, wu, wd, post_g, layer, tm, tf, mixer=None):
    m, d = x.shape
    ff = wg.shape[-1]
    gain = lambda width: _resident((None, 1, width), lambda i: (layer, 0, 0))
    rows = lambda width: pl.BlockSpec((tm, width), lambda i: (i, 0))
    mixer_specs = []
    if mixer is not None:
        mixer_specs = [rows(A_WIDTH), rows(B_WIDTH), gain(A_WIDTH),
                       _resident((None, A_WIDTH + B_WIDTH, d), lambda i: (layer, 0, 0)), gain(d)]
    return pl.pallas_call(
        functools.partial(_ffn_kernel, tf=tf, mixer=mixer is not None),
        grid=(m // tm,),
        in_specs=mixer_specs + [
            rows(d),
            gain(d),
            _resident((None, d, ff), lambda i: (layer, 0, 0)),
            _resident((None, d, ff), lambda i: (layer, 0, 0)),
            _resident((None, ff, d), lambda i: (layer, 0, 0)),
            gain(d),
        ],
        out_specs=rows(d),
        out_shape=jax.ShapeDtypeStruct((m, d), F32),
        scratch_shapes=[pltpu.VMEM((tm, d), BF16), pltpu.VMEM((tm, d), F32)],
        compiler_params=_cparams(1),
        name="mix_out_ffn" if mixer is not None else "ffn",
    )(*(mixer or ()), x, pre_g, wg, wu, wd, post_g)


def _mix_in_kernel(h_ref, g_ref, w_ref, z_ref, hn_ref):
    hn_ref[...] = _rms(h_ref[...], g_ref[...]).astype(BF16)
    for c in range(N_SPLITS):
        cols = slice(c * A_WIDTH, (c + 1) * A_WIDTH)
        z_ref[:, cols] = _dot(hn_ref[...], w_ref[:, cols])


def _mix_in(h, g, w_in, layer, tm):
    m, d = h.shape
    n = w_in.shape[-1]
    return pl.pallas_call(
        _mix_in_kernel,
        grid=(m // tm,),
        in_specs=[
            pl.BlockSpec((tm, d), lambda i: (i, 0)),
            _resident((None, 1, d), lambda i: (layer, 0, 0)),
            _resident((None, d, n), lambda i: (layer, 0, 0)),
        ],
        out_specs=pl.BlockSpec((tm, n), lambda i: (i, 0)),
        out_shape=jax.ShapeDtypeStruct((m, n), F32),
        scratch_shapes=[pltpu.VMEM((tm, d), BF16)],
        compiler_params=_cparams(1),
        name="mix_in",
    )(h, g, w_in)


def _attn_block(q, k_prev, k_own, v_prev, v_own, bias, first_head):
    w = q.shape[0]
    q = q * (LOG2_E / math.sqrt(A_HEAD_DIM))
    q2 = jnp.concatenate([jnp.where(first_head, q, 0.0), jnp.where(first_head, 0.0, q)], axis=0)
    kcat = jnp.concatenate([k_prev, k_own], axis=0).astype(BF16)
    vcat = jnp.concatenate([v_prev, v_own], axis=0).astype(BF16)
    s = _dot_nt(q2.astype(BF16), kcat) + jnp.concatenate([bias, bias], axis=0)
    m_row = jnp.max(s, axis=-1, keepdims=True)
    p = jnp.exp2(s - m_row)
    l_row = jnp.sum(p, axis=-1, keepdims=True)
    pv = _dot(p.astype(BF16), vcat)
    pick = lambda x: jnp.where(first_head, x[:w], x[w:])
    return (pick(jnp.broadcast_to(m_row, (2 * w, A_PAIR))),
            pick(jnp.broadcast_to(l_row, (2 * w, A_PAIR))), pick(pv))


def _attn_merge(m_o, l_o, acc_o, m_b, l_b, acc_b):
    m_n = jnp.maximum(m_o, m_b)
    a_o = jnp.exp2(m_o - m_n)
    a_b = jnp.exp2(m_b - m_n)
    return m_n, a_o * l_o + a_b * l_b, a_o * acc_o + a_b * acc_b


def _attn_prompt_kernel(q_ref, k_ref, v_ref, o_ref, m_ref, l_ref, acc_ref, q4_ref, k4_ref, v4_ref,
                        m4_ref, l4_ref, acc4_ref, bias_ref, *, unroll):
    w = WINDOW_STEPS
    seq = q_ref.shape[0]
    d1, d2 = DILATIONS[1], DILATIONS[2]
    ratio = d2 // d1
    per_res = seq // d1
    nb1 = per_res // w
    nb2 = seq // (d2 * w)
    first_head = lax.broadcasted_iota(jnp.int32, (w, A_PAIR), 1) < A_HEAD_DIM
    qi = lax.broadcasted_iota(jnp.int32, (w, 2 * w), 0)
    ki = lax.broadcasted_iota(jnp.int32, (w, 2 * w), 1)
    dist = w + qi - ki
    band = (dist >= 0) & (dist <= w)
    bias_ref[0] = jnp.where(band & (ki >= w), 0.0, -jnp.inf)
    bias_ref[1] = jnp.where(band, 0.0, -jnp.inf)

    def pattern0(c, carry):
        rows = pl.ds(pl.multiple_of(c * w, w), w)
        prev = pl.ds(pl.multiple_of(jnp.maximum(c - 1, 0) * w, w), w)
        m_b, l_b, acc_b = _attn_block(q_ref[rows, :], k_ref[prev, :], k_ref[rows, :], v_ref[prev, :],
                                      v_ref[rows, :], bias_ref[jnp.minimum(c, 1)], first_head)
        m_ref[rows, :] = m_b
        l_ref[rows, :] = l_b
        acc_ref[rows, :] = acc_b
        return carry

    lax.fori_loop(0, seq // w, pattern0, 0, unroll=unroll)

    for r in range(d1):
        dst = pl.ds(r * per_res, per_res)
        src = pl.ds(r, per_res, stride=d1)
        for nat, grouped in ((q_ref, q4_ref), (k_ref, k4_ref), (v_ref, v4_ref),
                             (m_ref, m4_ref), (l_ref, l4_ref), (acc_ref, acc4_ref)):
            grouped[dst, :] = nat[src, :]

    def pattern1(idx, carry):
        has_prev = jnp.minimum(idx & (nb1 - 1), 1)
        rows = pl.ds(pl.multiple_of(idx * w, w), w)
        prev = pl.ds(pl.multiple_of((idx - has_prev) * w, w), w)
        blk = _attn_block(q4_ref[rows, :], k4_ref[prev, :], k4_ref[rows, :], v4_ref[prev, :],
                          v4_ref[rows, :], bias_ref[has_prev], first_head)
        m_n, l_n, acc_n = _attn_merge(m4_ref[rows, :], l4_ref[rows, :], acc4_ref[rows, :], *blk)
        m4_ref[rows, :] = m_n
        l4_ref[rows, :] = l_n
        acc4_ref[rows, :] = acc_n
        return carry

    assert nb1 & (nb1 - 1) == 0 and nb2 & (nb2 - 1) == 0 and ratio & (ratio - 1) == 0
    lax.fori_loop(0, seq // w, pattern1, 0, unroll=unroll)

    def pattern2(idx, carry):
        a = idx & (ratio - 1)
        c = (idx >> (ratio.bit_length() - 1)) & (nb2 - 1)
        r = idx >> ((ratio * nb2).bit_length() - 1)
        has_prev = jnp.minimum(c, 1)
        start = r * per_res + c * (w * ratio) + a
        rows = pl.ds(start, w, stride=ratio)
        prev = pl.ds(start - has_prev * (w * ratio), w, stride=ratio)
        blk = _attn_block(q4_ref[rows, :], k4_ref[prev, :], k4_ref[rows, :], v4_ref[prev, :],
                          v4_ref[rows, :], bias_ref[has_prev], first_head)
        _, l_n, acc_n = _attn_merge(m4_ref[rows, :], l4_ref[rows, :], acc4_ref[rows, :], *blk)
        acc_ref[rows, :] = acc_n / l_n
        return carry

    lax.fori_loop(0, seq // w, pattern2, 0, unroll=unroll)

    for r in range(d1):
        o_ref[pl.ds(r, per_res, stride=d1), :] = acc_ref[pl.ds(r * per_res, per_res), :]


def _attn_prompt(z, batch, seq, unroll=8):
    assert DILATIONS[0] == 1 and len(DILATIONS) == 3 and DILATIONS[2] % DILATIONS[1] == 0
    assert seq % (DILATIONS[-1] * WINDOW_STEPS) == 0
    zv = z.reshape(batch, seq, N_SPLITS * A_WIDTH)
    n_pairs = A_WIDTH // A_PAIR
    spec = lambda split: pl.BlockSpec((None, seq, A_PAIR), lambda b, p: (b, 0, split * n_pairs + p))
    o = pl.pallas_call(
        functools.partial(_attn_prompt_kernel, unroll=unroll),
        grid=(batch, n_pairs),
        in_specs=[spec(0), spec(1), spec(2)],
        out_specs=pl.BlockSpec((None, seq, A_PAIR), lambda b, p: (b, 0, p)),
        out_shape=jax.ShapeDtypeStruct((batch, seq, A_WIDTH), F32),
        scratch_shapes=[pltpu.VMEM((seq, A_PAIR), F32)] * 9
        + [pltpu.VMEM((2, WINDOW_STEPS, 2 * WINDOW_STEPS), F32)],
        compiler_params=_cparams(2),
        name="attn_prompt",
    )(zv, zv, zv)
    return o.reshape(batch * seq, A_WIDTH)


def _pattern_count(delta):
    cnt = jnp.zeros(delta.shape, F32)
    for dil in DILATIONS:
        hit = (delta >= 0) & ((delta & (dil - 1)) == 0) & (delta <= dil * WINDOW_STEPS)
        cnt = cnt + hit.astype(F32)
    return cnt


def _attn_sample_kernel(q_ref, kn_ref, vn_ref, kt_ref, vt_ref, o_ref):
    t = q_ref.shape[0]
    hq = A_HEADS * t
    n_cache = kt_ref.shape[-1]
    assert t & (t - 1) == 0
    t_shift = t.bit_length() - 1

    def weights(n_cols, first_pos):
        query = lax.broadcasted_iota(jnp.int32, (hq, n_cols), 0) & (t - 1)
        key = first_pos + lax.broadcasted_iota(jnp.int32, (hq, n_cols), 1)
        return _pattern_count(n_cache + query - key)

    w_cache = weights(n_cache, 0)
    w_new = weights(t, n_cache)

    row_head = lax.broadcasted_iota(jnp.int32, (hq, A_WIDTH), 0) >> t_shift
    lane_head = lax.broadcasted_iota(jnp.int32, (hq, A_WIDTH), 1) >> (A_HEAD_DIM.bit_length() - 1)
    own_head = row_head == lane_head
    q = q_ref[...] * (1.0 / math.sqrt(A_HEAD_DIM))
    q_bd = jnp.where(own_head, jnp.concatenate([q] * A_HEADS, axis=0), 0.0).astype(BF16)

    kt = kt_ref[...].reshape(A_WIDTH, n_cache).astype(BF16)
    vt = vt_ref[...].reshape(A_WIDTH, n_cache).astype(BF16)
    s_c = jnp.where(w_cache > 0, _dot(q_bd, kt), -jnp.inf)
    s_n = jnp.where(w_new > 0, _dot_nt(q_bd, kn_ref[...].astype(BF16)), -jnp.inf)
    m = jnp.maximum(jnp.max(s_c, axis=-1, keepdims=True), jnp.max(s_n, axis=-1, keepdims=True))
    p_c = w_cache * jnp.exp(s_c - m)
    p_n = w_new * jnp.exp(s_n - m)
    l = jnp.sum(p_c, axis=-1, keepdims=True) + jnp.sum(p_n, axis=-1, keepdims=True)
    r = (_dot_nt(p_c.astype(BF16), vt) + _dot(p_n.astype(BF16), vn_ref[...].astype(BF16))) / l
    r = jnp.where(own_head, r, 0.0)
    out = r[0:t]
    for h in range(1, A_HEADS):
        out = out + r[h * t:(h + 1) * t]
    o_ref[...] = out


def _attn_sample(z, cache_k, cache_v, layer, batch, t):
    n_cache = cache_k.shape[2]
    zv = z.reshape(batch, t, N_SPLITS * A_WIDTH)
    ckt = jnp.transpose(cache_k, (0, 1, 3, 4, 2))
    cvt = jnp.transpose(cache_v, (0, 1, 3, 4, 2))
    zspec = lambda col: pl.BlockSpec((None, t, A_WIDTH), lambda b: (b, 0, col))
    cspec = pl.BlockSpec((None, None, A_HEADS, A_HEAD_DIM, n_cache), lambda b: (layer, b, 0, 0, 0))
    o = pl.pallas_call(
        _attn_sample_kernel,
        grid=(batch,),
        in_specs=[zspec(0), zspec(1), zspec(2), cspec, cspec],
        out_specs=pl.BlockSpec((None, t, A_WIDTH), lambda b: (b, 0, 0)),
        out_shape=jax.ShapeDtypeStruct((batch, t, A_WIDTH), F32),
        compiler_params=_cparams(1),
        name="attn_sample",
    )(zv, zv, zv, ckt, cvt)
    return o.reshape(batch * t, A_WIDTH)


def _kv_format_kernel(*refs, depth):
    k_refs, v_refs = refs[:depth], refs[depth:2 * depth]
    ko_ref, vo_ref = refs[2 * depth:]
    tm = ko_ref.shape[-1]
    for lyr in range(depth):
        @pl.when(pl.program_id(0) == lyr)
        def _(lyr=lyr):
            for src, dst in ((k_refs[lyr], ko_ref), (v_refs[lyr], vo_ref)):
                dst[...] = src[...].T.reshape(A_HEADS, A_HEAD_DIM, tm)


def _kv_format(zs, batch, seq, keep, tm):
    depth = len(zs)
    zvs = [z.reshape(batch, seq, N_SPLITS * A_WIDTH) for z in zs]
    t0, nt = (seq - keep) // tm, keep // tm

    def in_spec(lyr, split):
        def index(l, b, t):
            before, after = l < lyr, l > lyr
            bb = jnp.where(before, 0, jnp.where(after, batch - 1, b))
            tt = jnp.where(before, 0, jnp.where(after, nt - 1, t))
            return (bb, t0 + tt, split)
        return pl.BlockSpec((None, tm, A_WIDTH), index)

    out_spec = pl.BlockSpec((None, None, A_HEADS, A_HEAD_DIM, tm), lambda l, b, t: (l, b, 0, 0, t))
    shape = jax.ShapeDtypeStruct((depth, batch, A_HEADS, A_HEAD_DIM, keep), F32)
    kt, vt = pl.pallas_call(
        functools.partial(_kv_format_kernel, depth=depth),
        grid=(depth, batch, nt),
        in_specs=[in_spec(l, 1) for l in range(depth)] + [in_spec(l, 2) for l in range(depth)],
        out_specs=[out_spec, out_spec],
        out_shape=[shape, shape],
        compiler_params=_cparams(3, parallel_first=False),
        name="kv_format",
    )(*zvs, *zvs)
    return jnp.transpose(kt, (0, 1, 4, 2, 3)), jnp.transpose(vt, (0, 1, 4, 2, 3))


def _hgrn_kernel(xq_ref, xf_ref, xi_ref, xg_ref, lbl_ref, g_ref, s0_ref, o_ref, sfin_ref, st_ref, g2_scr, k_scr,
                 *, layer, chunk, sub):
    t_id = pl.program_id(1)
    n_seqs = xq_ref.shape[0]
    heads = [(bi, h) for bi in range(n_seqs) for h in range(B_HEADS)]

    @pl.when(t_id == 0)
    def _():
        for bi, h in heads:
            st_ref[bi * B_HEADS + h] = s0_ref[bi, h].T

    logits = lbl_ref[...]
    e = jnp.exp(logits - jnp.max(logits, axis=0, keepdims=True))
    soft = e / jnp.sum(e, axis=0, keepdims=True)
    lb_all = soft[0:1] - soft[0:1]
    for l in range(1, layer + 1):
        lb_all = lb_all + soft[l:l + 1]

    n_chunks = xq_ref.shape[1] // chunk
    n_sub = chunk // sub
    t_idx = lax.broadcasted_iota(jnp.int32, (chunk, chunk), 0)
    s_idx = lax.broadcasted_iota(jnp.int32, (chunk, chunk), 1)
    tri = (t_idx >= s_idx).astype(BF16)
    row_c = lax.broadcasted_iota(jnp.int32, (chunk, B_DIM), 0)
    sub_t = lax.broadcasted_iota(jnp.int32, (sub, chunk), 0)
    sub_c = lax.broadcasted_iota(jnp.int32, (sub, chunk), 1)
    diag_code = [jnp.where((sub_c >= i * sub) & (sub_c - i * sub <= sub_t), sub_c - i * sub, -1)
                 for i in range(n_sub)]
    halves = []
    half = chunk // 2
    while half >= sub:
        halves.append(half)
        half //= 2

    for ci in range(n_chunks):
        rows = slice(ci * chunk, (ci + 1) * chunk)
        for bi, h in heads:
            cols = slice(h * B_DIM, (h + 1) * B_DIM)
            xq = xq_ref[bi, rows, cols]
            xg = xg_ref[bi, rows, cols]
            v = xi_ref[bi, rows, cols]
            lb = lb_all[:, cols]
            qh = xq * _sigmoid(xq)
            f = lb + (1.0 - lb) * _sigmoid(xf_ref[bi, rows, cols])
            logf = jnp.log(f)
            kh = 1.0 - f

            hi = logf.astype(BF16)
            r1 = logf - hi.astype(F32)
            mid = r1.astype(BF16)
            lo = (r1 - mid.astype(F32)).astype(BF16)
            g2 = (_dot(tri, hi) + _dot(tri, mid) + _dot(tri, lo)) * LOG2_E

            st_slot = bi * B_HEADS + h
            st = st_ref[st_slot]
            o = _dot_nt((qh * jnp.exp2(g2)).astype(BF16), st.astype(BF16))

            slot = ci * len(heads) + st_slot
            g2_scr[slot] = g2
            k_scr[slot] = kh
            a_rows = []
            for i in range(n_sub):
                r0 = i * sub
                gi = g2[r0:r0 + sub]
                qi = qh[r0:r0 + sub]
                a_i = jnp.zeros((sub, chunk), F32)
                for s in range(sub):
                    dec = jnp.exp2(gi - g2_scr[slot, pl.ds(r0 + s, 1), :])
                    col = jnp.sum(qi * k_scr[slot, pl.ds(r0 + s, 1), :] * dec, axis=-1, keepdims=True)
                    a_i = jnp.where(diag_code[i] == s, col, a_i)
                a_rows.append(a_i)
            a = a_rows[0] if n_sub == 1 else jnp.concatenate(a_rows, axis=0)
            for half in halves:
                shift = half.bit_length() - 1
                bounds = [jnp.broadcast_to(g2[r0 + half - 1:r0 + half], (2 * half, B_DIM))
                          for r0 in range(0, chunk, 2 * half)]
                bnd = bounds[0] if len(bounds) == 1 else jnp.concatenate(bounds, axis=0)
                e_half = jnp.exp2(-jnp.abs(g2 - bnd))
                upper = ((row_c >> shift) & 1) == 1
                qo = jnp.where(upper, qh * e_half, 0.0).astype(BF16)
                kk = jnp.where(upper, 0.0, kh * e_half).astype(BF16)
                a_half = _dot_nt(qo, kk)
                if 2 * half < chunk:
                    a_half = jnp.where((t_idx >> (shift + 1)) == (s_idx >> (shift + 1)), a_half, 0.0)
                a = a + a_half
            v16 = v.astype(BF16)
            o = o + _dot(a.astype(BF16), v16)

            g_last = g2[chunk - 1:chunk]
            kd = (kh * jnp.exp2(g_last - g2)).astype(BF16)
            st_ref[st_slot] = jnp.exp2(g_last) * st + _dot_tn(v16, kd)

            o_ref[bi, rows, cols] = _rms(o, g_ref[...]) * (xg * _sigmoid(xg))

    @pl.when(t_id == pl.num_programs(1) - 1)
    def _():
        for bi, h in heads:
            sfin_ref[bi, h] = st_ref[bi * B_HEADS + h].T


def _hgrn(z, lb_logits, norm_g, s0, s0_layer, layer, batch, seq, bb, tt, chunk, sub):
    zv = z.reshape(batch, seq, N_SPLITS * A_WIDTH)
    zspec = lambda col: pl.BlockSpec((bb, tt, B_WIDTH), lambda b, t: (b, t, col))
    sspec = pl.BlockSpec((bb, B_HEADS, B_DIM, B_DIM), lambda b, t: (b, 0, 0, 0))
    s0spec = pl.BlockSpec((None, bb, B_HEADS, B_DIM, B_DIM), lambda b, t: (s0_layer, b, 0, 0, 0))
    depth = lb_logits.shape[0]
    o, sfin = pl.pallas_call(
        functools.partial(_hgrn_kernel, layer=layer, chunk=chunk, sub=sub),
        grid=(batch // bb, seq // tt),
        in_specs=[zspec(3), zspec(4), zspec(5), zspec(6),
                  pl.BlockSpec((depth, B_WIDTH), lambda b, t: (0, 0)),
                  pl.BlockSpec((None, 1, B_DIM), lambda b, t: (layer, 0, 0)),
                  s0spec],
        out_specs=[pl.BlockSpec((bb, tt, B_WIDTH), lambda b, t: (b, t, 0)), sspec],
        out_shape=[jax.ShapeDtypeStruct((batch, seq, B_WIDTH), F32),
                   jax.ShapeDtypeStruct((batch, B_HEADS, B_DIM, B_DIM), F32)],
        scratch_shapes=[pltpu.VMEM((bb * B_HEADS, B_DIM, B_DIM), F32)]
        + [pltpu.VMEM((tt // chunk * bb * B_HEADS, chunk, B_DIM), F32)] * 2,
        compiler_params=_cparams(2),
        name="hgrn",
    )(zv, zv, zv, zv, lb_logits, norm_g, s0)
    return o.reshape(batch * seq, B_WIDTH), sfin


def kernel(x_prompt, x_sample, cache_attn_k, cache_attn_v, state_hgrn, ff1_pre_g, ff1_w_gate, ff1_w_up, ff1_w_down, ff1_post_g, mix_pre_g, w_in, attn_norm_g, hgrn_lb_logits, hgrn_norm_g, w_out, mix_post_g, ff2_pre_g, ff2_w_gate, ff2_w_up, ff2_w_down, ff2_post_g):
    batch, seq, d = x_prompt.shape
    dec_batch, dec_seq, _ = x_sample.shape
    depth = w_in.shape[0]
    keep = min(DILATIONS[-1] * WINDOW_STEPS, seq)

    row = lambda g: g.reshape(depth, 1, g.shape[-1])
    bf = lambda w: w.astype(BF16)
    ff1 = (row(ff1_pre_g), bf(ff1_w_gate), bf(ff1_w_up), bf(ff1_w_down), row(ff1_post_g))
    ff2 = (row(ff2_pre_g), bf(ff2_w_gate), bf(ff2_w_up), bf(ff2_w_down), row(ff2_post_g))
    mix_pre_g, attn_norm_g, hgrn_norm_g, mix_post_g = map(row, (mix_pre_g, attn_norm_g, hgrn_norm_g, mix_post_g))
    w_in16, w_out16 = bf(w_in), bf(w_out)
    zero_state = jnp.zeros((1, batch, B_HEADS, B_DIM, B_DIM), F32)

    hp = x_prompt.reshape(batch * seq, d)
    hs = x_sample.reshape(dec_batch * dec_seq, d)
    tm_p, tm_s, tf = 512, dec_batch * dec_seq, 256
    zps = []
    outs = {k: [] for k in ("sp", "ks", "vs", "ss")}
    for l in range(depth):
        hp = _ffn(hp, *ff1, l, tm_p, tf)
        zp = _mix_in(hp, mix_pre_g, w_in16, l, 512)
        att = _attn_prompt(zp, batch, seq)
        ob, sp = _hgrn(zp, hgrn_lb_logits, hgrn_norm_g, zero_state, 0, l, batch, seq,
                       1, 4 * HGRN_CHUNK, HGRN_CHUNK, HGRN_SUB)
        hp = _ffn(hp, *ff2, l, tm_p, tf, mixer=(att, ob, attn_norm_g, w_out16, mix_post_g))
        zps.append(zp)
        outs["sp"].append(sp)

        hs = _ffn(hs, *ff1, l, tm_s, tf)
        zs = _mix_in(hs, mix_pre_g, w_in16, l, tm_s)
        att_s = _attn_sample(zs, cache_attn_k, cache_attn_v, l, dec_batch, dec_seq)
        ob_s, ss = _hgrn(zs, hgrn_lb_logits, hgrn_norm_g, state_hgrn, l, l, dec_batch, dec_seq,
                         4, dec_seq, dec_seq, dec_seq)
        hs = _ffn(hs, *ff2, l, tm_s, tf, mixer=(att_s, ob_s, attn_norm_g, w_out16, mix_post_g))
        zs3 = zs.reshape(dec_batch, dec_seq, N_SPLITS, A_HEADS, A_HEAD_DIM)
        outs["ks"].append(zs3[:, :, 1])
        outs["vs"].append(zs3[:, :, 2])
        outs["ss"].append(ss)

    kp, vp = _kv_format(zps, batch, seq, keep, 512)
    stack = lambda k: jnp.stack(outs[k])
    return (hp.reshape(batch, seq, d), hs.reshape(dec_batch, dec_seq, d),
            kp, vp, stack("sp"), stack("ks"), stack("vs"), stack("ss"))
```

```python
import functools
import math

import jax
import jax.numpy as jnp
from jax import lax
from jax.experimental import pallas as pl
from jax.experimental.pallas import tpu as pltpu

F32 = jnp.float32
BF16 = jnp.bfloat16

NORM_EPS = 1e-6
LOG2_E = math.log2(math.e)
A_HEADS = 8
A_HEAD_DIM = 64
A_WIDTH = A_HEADS * A_HEAD_DIM
A_PAIR = 2 * A_HEAD_DIM
B_HEADS = 4
B_DIM = 128
B_WIDTH = B_HEADS * B_DIM
DILATIONS = (1, 4, 16)
WINDOW_STEPS = 128
N_SPLITS = 7
HGRN_CHUNK = 64
HGRN_SUB = 8

V7X_VMEM_LIMIT_BYTES = 56 * 1024 * 1024


def _cparams(n_grid_axes, parallel_first=True):
    first = "parallel" if parallel_first else "arbitrary"
    sem = (first,) + ("arbitrary",) * (n_grid_axes - 1)
    return pltpu.CompilerParams(dimension_semantics=sem,
                                vmem_limit_bytes=V7X_VMEM_LIMIT_BYTES)


def _rms(x, g):
    return x * lax.rsqrt(jnp.mean(x * x, axis=-1, keepdims=True) + NORM_EPS) * g


def _sigmoid(x):
    return 1.0 / (1.0 + jnp.exp(-x))


def _dot(a, b):
    return jnp.dot(a, b, preferred_element_type=F32)


def _dot_nt(a, b):
    return lax.dot_general(a, b, (((1,), (1,)), ((), ())), preferred_element_type=F32)


def _dot_tn(a, b):
    return lax.dot_general(a, b, (((0,), (0,)), ((), ())), preferred_element_type=F32)


def _ffn_kernel(*refs, tf, mixer, proj):
    if mixer:
        att_ref, ob_ref, attn_g_ref, wo_ref, mix_post_g_ref = refs[:5]
        refs = refs[5:]
    x_ref, pre_g_ref, wg_ref, wu_ref, wd_ref, post_g_ref = refs[:6]
    refs = refs[6:]
    if proj:
        mix_pre_g_ref, w_in_ref, o_ref, z_ref, xn_ref, acc_ref = refs
    else:
        o_ref, xn_ref, acc_ref = refs
    if mixer:
        oa = _rms(att_ref[...], attn_g_ref[...]).astype(BF16)
        m = _dot(oa, wo_ref[:A_WIDTH, :]) + _dot(ob_ref[...].astype(BF16), wo_ref[A_WIDTH:, :])
        o_ref[...] = x_ref[...] + _rms(m, mix_post_g_ref[...])
        res_ref = o_ref
    else:
        res_ref = x_ref
    xn_ref[...] = _rms(res_ref[...], pre_g_ref[...]).astype(BF16)
    ff = wg_ref.shape[-1]
    for c in range(ff // tf):
        cols = slice(c * tf, (c + 1) * tf)
        xn = xn_ref[...]
        g = _dot(xn, wg_ref[:, cols])
        u = _dot(xn, wu_ref[:, cols])
        a = (g * _sigmoid(g) * u).astype(BF16)
        part = _dot(a, wd_ref[cols, :])
        if c == 0:
            acc_ref[...] = part
        else:
            acc_ref[...] += part
    o_ref[...] = res_ref[...] + 0.5 * _rms(acc_ref[...], post_g_ref[...])
    if proj:
        xn_ref[...] = _rms(o_ref[...], mix_pre_g_ref[...]).astype(BF16)
        for c in range(N_SPLITS):
            cols = slice(c * A_WIDTH, (c + 1) * A_WIDTH)
            z_ref[:, cols] = _dot(xn_ref[...], w_in_ref[:, cols])


def _resident(block_shape, index_map):
    return pl.BlockSpec(block_shape, index_map, pipeline_mode=pl.Buffered(1))


def _ffn(x, pre_g, wg, wu, wd, post_g, layer, tm, tf, mixer=None, proj=None):
    m, d = x.shape
    ff = wg.shape[-1]
    gain = lambda width: _resident((None, 1, width), lambda i: (layer, 0, 0))
    rows = lambda width: pl.BlockSpec((tm, width), lambda i: (i, 0))
    weight = lambda k, n: _resident((None, k, n), lambda i: (layer, 0, 0))
    mixer_specs, proj_specs = [], []
    out_specs, out_shape = rows(d), jax.ShapeDtypeStruct((m, d), F32)
    if mixer is not None:
        mixer_specs = [rows(A_WIDTH), rows(B_WIDTH), gain(A_WIDTH), weight(A_WIDTH + B_WIDTH, d), gain(d)]
    if proj is not None:
        n = proj[1].shape[-1]
        proj_specs = [gain(d), weight(d, n)]
        out_specs, out_shape = [out_specs, rows(n)], [out_shape, jax.ShapeDtypeStruct((m, n), F32)]
    return pl.pallas_call(
        functools.partial(_ffn_kernel, tf=tf, mixer=mixer is not None, proj=proj is not None),
        grid=(m // tm,),
        in_specs=mixer_specs + [rows(d), gain(d), weight(d, ff), weight(d, ff), weight(ff, d), gain(d)] + proj_specs,
        out_specs=out_specs,
        out_shape=out_shape,
        scratch_shapes=[pltpu.VMEM((tm, d), BF16), pltpu.VMEM((tm, d), F32)],
        compiler_params=_cparams(1),
        name=("mix_out_" if mixer is not None else "") + "ffn" + ("_mix_in" if proj is not None else ""),
    )(*(mixer or ()), x, pre_g, wg, wu, wd, post_g, *(proj or ()))


def _attn_block(q, k_prev, k_own, v_prev, v_own, bias, first_head):
    w = q.shape[0]
    q = q * (LOG2_E / math.sqrt(A_HEAD_DIM))
    q2 = jnp.concatenate([jnp.where(first_head, q, 0.0), jnp.where(first_head, 0.0, q)], axis=0)
    kcat = jnp.concatenate([k_prev, k_own], axis=0).astype(BF16)
    vcat = jnp.concatenate([v_prev, v_own], axis=0).astype(BF16)
    s = _dot_nt(q2.astype(BF16), kcat) + jnp.concatenate([bias, bias], axis=0)
    m_row = jnp.max(s, axis=-1, keepdims=True)
    p = jnp.exp2(s - m_row)
    l_row = jnp.sum(p, axis=-1, keepdims=True)
    pv = _dot(p.astype(BF16), vcat)
    pick = lambda x: jnp.where(first_head, x[:w], x[w:])
    return (pick(jnp.broadcast_to(m_row, (2 * w, A_PAIR))),
            pick(jnp.broadcast_to(l_row, (2 * w, A_PAIR))), pick(pv))


def _attn_merge(m_o, l_o, acc_o, m_b, l_b, acc_b):
    m_n = jnp.maximum(m_o, m_b)
    a_o = jnp.exp2(m_o - m_n)
    a_b = jnp.exp2(m_b - m_n)
    return m_n, a_o * l_o + a_b * l_b, a_o * acc_o + a_b * acc_b


def _attn_prompt_kernel(q_ref, k_ref, v_ref, o_ref, m_ref, l_ref, acc_ref, q4_ref, k4_ref, v4_ref,
                        m4_ref, l4_ref, acc4_ref, bias_ref, *, unroll):
    w = WINDOW_STEPS
    seq = q_ref.shape[0]
    d1, d2 = DILATIONS[1], DILATIONS[2]
    ratio = d2 // d1
    per_res = seq // d1
    nb1 = per_res // w
    nb2 = seq // (d2 * w)
    first_head = lax.broadcasted_iota(jnp.int32, (w, A_PAIR), 1) < A_HEAD_DIM
    qi = lax.broadcasted_iota(jnp.int32, (w, 2 * w), 0)
    ki = lax.broadcasted_iota(jnp.int32, (w, 2 * w), 1)
    dist = w + qi - ki
    band = (dist >= 0) & (dist <= w)
    bias_ref[0] = jnp.where(band & (ki >= w), 0.0, -jnp.inf)
    bias_ref[1] = jnp.where(band, 0.0, -jnp.inf)

    def pattern0(c, carry):
        rows = pl.ds(pl.multiple_of(c * w, w), w)
        prev = pl.ds(pl.multiple_of(jnp.maximum(c - 1, 0) * w, w), w)
        m_b, l_b, acc_b = _attn_block(q_ref[rows, :], k_ref[prev, :], k_ref[rows, :], v_ref[prev, :],
                                      v_ref[rows, :], bias_ref[jnp.minimum(c, 1)], first_head)
        m_ref[rows, :] = m_b
        l_ref[rows, :] = l_b
        acc_ref[rows, :] = acc_b
        return carry

    lax.fori_loop(0, seq // w, pattern0, 0, unroll=unroll)

    for r in range(d1):
        dst = pl.ds(r * per_res, per_res)
        src = pl.ds(r, per_res, stride=d1)
        for nat, grouped in ((q_ref, q4_ref), (k_ref, k4_ref), (v_ref, v4_ref),
                             (m_ref, m4_ref), (l_ref, l4_ref), (acc_ref, acc4_ref)):
            grouped[dst, :] = nat[src, :]

    def pattern1(idx, carry):
        has_prev = jnp.minimum(idx & (nb1 - 1), 1)
        rows = pl.ds(pl.multiple_of(idx * w, w), w)
        prev = pl.ds(pl.multiple_of((idx - has_prev) * w, w), w)
        blk = _attn_block(q4_ref[rows, :], k4_ref[prev, :], k4_ref[rows, :], v4_ref[prev, :],
                          v4_ref[rows, :], bias_ref[has_prev], first_head)
        m_n, l_n, acc_n = _attn_merge(m4_ref[rows, :], l4_ref[rows, :], acc4_ref[rows, :], *blk)
        m4_ref[rows, :] = m_n
        l4_ref[rows, :] = l_n
        acc4_ref[rows, :] = acc_n
        return carry

    assert nb1 & (nb1 - 1) == 0 and nb2 & (nb2 - 1) == 0 and ratio & (ratio - 1) == 0
    lax.fori_loop(0, seq // w, pattern1, 0, unroll=unroll)

    def pattern2(idx, carry):
        a = idx & (ratio - 1)
        c = (idx >> (ratio.bit_length() - 1)) & (nb2 - 1)
        r = idx >> ((ratio * nb2).bit_length() - 1)
        has_prev = jnp.minimum(c, 1)
        start = r * per_res + c * (w * ratio) + a
        rows = pl.ds(start, w, stride=ratio)
        prev = pl.ds(start - has_prev * (w * ratio), w, stride=ratio)
        blk = _attn_block(q4_ref[rows, :], k4_ref[prev, :], k4_ref[rows, :], v4_ref[prev, :],
                          v4_ref[rows, :], bias_ref[has_prev], first_head)
        _, l_n, acc_n = _attn_merge(m4_ref[rows, :], l4_ref[rows, :], acc4_ref[rows, :], *blk)
        acc_ref[rows, :] = acc_n / l_n
        return carry

    lax.fori_loop(0, seq // w, pattern2, 0, unroll=unroll)

    for r in range(d1):
        o_ref[pl.ds(r, per_res, stride=d1), :] = acc_ref[pl.ds(r * per_res, per_res), :]


def _attn_prompt(z, batch, seq, unroll=8):
    assert DILATIONS[0] == 1 and len(DILATIONS) == 3 and DILATIONS[2] % DILATIONS[1] == 0
    assert seq % (DILATIONS[-1] * WINDOW_STEPS) == 0
    zv = z.reshape(batch, seq, N_SPLITS * A_WIDTH)
    n_pairs = A_WIDTH // A_PAIR
    spec = lambda split: pl.BlockSpec((None, seq, A_PAIR), lambda b, p: (b, 0, split * n_pairs + p))
    o = pl.pallas_call(
        functools.partial(_attn_prompt_kernel, unroll=unroll),
        grid=(batch, n_pairs),
        in_specs=[spec(0), spec(1), spec(2)],
        out_specs=pl.BlockSpec((None, seq, A_PAIR), lambda b, p: (b, 0, p)),
        out_shape=jax.ShapeDtypeStruct((batch, seq, A_WIDTH), F32),
        scratch_shapes=[pltpu.VMEM((seq, A_PAIR), F32)] * 9
        + [pltpu.VMEM((2, WINDOW_STEPS, 2 * WINDOW_STEPS), F32)],
        compiler_params=_cparams(2),
        name="attn_prompt",
    )(zv, zv, zv)
    return o.reshape(batch * seq, A_WIDTH)


def _pattern_count(delta):
    cnt = jnp.zeros(delta.shape, F32)
    for dil in DILATIONS:
        hit = (delta >= 0) & ((delta & (dil - 1)) == 0) & (delta <= dil * WINDOW_STEPS)
        cnt = cnt + hit.astype(F32)
    return cnt


def _attn_sample_kernel(q_ref, kn_ref, vn_ref, kt_ref, vt_ref, o_ref):
    t = q_ref.shape[0]
    hq = A_HEADS * t
    n_cache = kt_ref.shape[-1]
    assert t & (t - 1) == 0
    t_shift = t.bit_length() - 1

    def weights(n_cols, first_pos):
        query = lax.broadcasted_iota(jnp.int32, (hq, n_cols), 0) & (t - 1)
        key = first_pos + lax.broadcasted_iota(jnp.int32, (hq, n_cols), 1)
        return _pattern_count(n_cache + query - key)

    w_cache = weights(n_cache, 0)
    w_new = weights(t, n_cache)

    row_head = lax.broadcasted_iota(jnp.int32, (hq, A_WIDTH), 0) >> t_shift
    lane_head = lax.broadcasted_iota(jnp.int32, (hq, A_WIDTH), 1) >> (A_HEAD_DIM.bit_length() - 1)
    own_head = row_head == lane_head
    q = q_ref[...] * (1.0 / math.sqrt(A_HEAD_DIM))
    q_bd = jnp.where(own_head, jnp.concatenate([q] * A_HEADS, axis=0), 0.0).astype(BF16)

    kt = kt_ref[...].reshape(A_WIDTH, n_cache).astype(BF16)
    vt = vt_ref[...].reshape(A_WIDTH, n_cache).astype(BF16)
    s_c = jnp.where(w_cache > 0, _dot(q_bd, kt), -jnp.inf)
    s_n = jnp.where(w_new > 0, _dot_nt(q_bd, kn_ref[...].astype(BF16)), -jnp.inf)
    m = jnp.maximum(jnp.max(s_c, axis=-1, keepdims=True), jnp.max(s_n, axis=-1, keepdims=True))
    p_c = w_cache * jnp.exp(s_c - m)
    p_n = w_new * jnp.exp(s_n - m)
    l = jnp.sum(p_c, axis=-1, keepdims=True) + jnp.sum(p_n, axis=-1, keepdims=True)
    r = (_dot_nt(p_c.astype(BF16), vt) + _dot(p_n.astype(BF16), vn_ref[...].astype(BF16))) / l
    r = jnp.where(own_head, r, 0.0)
    out = r[0:t]
    for h in range(1, A_HEADS):
        out = out + r[h * t:(h + 1) * t]
    o_ref[...] = out


def _attn_sample(z, cache_k, cache_v, layer, batch, t):
    n_cache = cache_k.shape[2]
    zv = z.reshape(batch, t, N_SPLITS * A_WIDTH)
    ckt = jnp.transpose(cache_k, (0, 1, 3, 4, 2))
    cvt = jnp.transpose(cache_v, (0, 1, 3, 4, 2))
    zspec = lambda col: pl.BlockSpec((None, t, A_WIDTH), lambda b: (b, 0, col))
    cspec = pl.BlockSpec((None, None, A_HEADS, A_HEAD_DIM, n_cache), lambda b: (layer, b, 0, 0, 0))
    o = pl.pallas_call(
        _attn_sample_kernel,
        grid=(batch,),
        in_specs=[zspec(0), zspec(1), zspec(2), cspec, cspec],
        out_specs=pl.BlockSpec((None, t, A_WIDTH), lambda b: (b, 0, 0)),
        out_shape=jax.ShapeDtypeStruct((batch, t, A_WIDTH), F32),
        compiler_params=_cparams(1),
        name="attn_sample",
    )(zv, zv, zv, ckt, cvt)
    return o.reshape(batch * t, A_WIDTH)


def _kv_format_kernel(*refs, depth):
    k_refs, v_refs = refs[:depth], refs[depth:2 * depth]
    ko_ref, vo_ref = refs[2 * depth:]
    tm = ko_ref.shape[-1]
    for lyr in range(depth):
        @pl.when(pl.program_id(0) == lyr)
        def _(lyr=lyr):
            for src, dst in ((k_refs[lyr], ko_ref), (v_refs[lyr], vo_ref)):
                dst[...] = src[...].T.reshape(A_HEADS, A_HEAD_DIM, tm)


def _kv_format(zs, batch, seq, keep, tm):
    depth = len(zs)
    zvs = [z.reshape(batch, seq, N_SPLITS * A_WIDTH) for z in zs]
    t0, nt = (seq - keep) // tm, keep // tm

    def in_spec(lyr, split):
        def index(l, b, t):
            before, after = l < lyr, l > lyr
            bb = jnp.where(before, 0, jnp.where(after, batch - 1, b))
            tt = jnp.where(before, 0, jnp.where(after, nt - 1, t))
            return (bb, t0 + tt, split)
        return pl.BlockSpec((None, tm, A_WIDTH), index)

    out_spec = pl.BlockSpec((None, None, A_HEADS, A_HEAD_DIM, tm), lambda l, b, t: (l, b, 0, 0, t))
    shape = jax.ShapeDtypeStruct((depth, batch, A_HEADS, A_HEAD_DIM, keep), F32)
    kt, vt = pl.pallas_call(
        functools.partial(_kv_format_kernel, depth=depth),
        grid=(depth, batch, nt),
        in_specs=[in_spec(l, 1) for l in range(depth)] + [in_spec(l, 2) for l in range(depth)],
        out_specs=[out_spec, out_spec],
        out_shape=[shape, shape],
        compiler_params=_cparams(3, parallel_first=False),
        name="kv_format",
    )(*zvs, *zvs)
    return jnp.transpose(kt, (0, 1, 4, 2, 3)), jnp.transpose(vt, (0, 1, 4, 2, 3))


def _hgrn_kernel(xq_ref, xf_ref, xi_ref, xg_ref, lbl_ref, g_ref, s0_ref, o_ref, sfin_ref, st_ref, g2_scr, k_scr,
                 *, layer, chunk, sub):
    t_id = pl.program_id(1)
    n_seqs = xq_ref.shape[0]
    heads = [(bi, h) for bi in range(n_seqs) for h in range(B_HEADS)]

    @pl.when(t_id == 0)
    def _():
        for bi, h in heads:
            st_ref[bi * B_HEADS + h] = s0_ref[bi, h].T

    logits = lbl_ref[...]
    e = jnp.exp(logits - jnp.max(logits, axis=0, keepdims=True))
    soft = e / jnp.sum(e, axis=0, keepdims=True)
    lb_all = soft[0:1] - soft[0:1]
    for l in range(1, layer + 1):
        lb_all = lb_all + soft[l:l + 1]

    n_chunks = xq_ref.shape[1] // chunk
    n_sub = chunk // sub
    t_idx = lax.broadcasted_iota(jnp.int32, (chunk, chunk), 0)
    s_idx = lax.broadcasted_iota(jnp.int32, (chunk, chunk), 1)
    tri = (t_idx >= s_idx).astype(BF16)
    row_c = lax.broadcasted_iota(jnp.int32, (chunk, B_DIM), 0)
    sub_t = lax.broadcasted_iota(jnp.int32, (sub, chunk), 0)
    sub_c = lax.broadcasted_iota(jnp.int32, (sub, chunk), 1)
    diag_code = [jnp.where((sub_c >= i * sub) & (sub_c - i * sub <= sub_t), sub_c - i * sub, -1)
                 for i in range(n_sub)]
    halves = []
    half = chunk // 2
    while half >= sub:
        halves.append(half)
        half //= 2

    for ci in range(n_chunks):
        rows = slice(ci * chunk, (ci + 1) * chunk)
        for bi, h in heads:
            cols = slice(h * B_DIM, (h + 1) * B_DIM)
            xq = xq_ref[bi, rows, cols]
            xg = xg_ref[bi, rows, cols]
            v = xi_ref[bi, rows, cols]
            lb = lb_all[:, cols]
            qh = xq * _sigmoid(xq)
            f = lb + (1.0 - lb) * _sigmoid(xf_ref[bi, rows, cols])
            logf = jnp.log(f)
            kh = 1.0 - f

            hi = logf.astype(BF16)
            r1 = logf - hi.astype(F32)
            mid = r1.astype(BF16)
            lo = (r1 - mid.astype(F32)).astype(BF16)
            g2 = (_dot(tri, hi) + _dot(tri, mid) + _dot(tri, lo)) * LOG2_E

            st_slot = bi * B_HEADS + h
            st = st_ref[st_slot]
            o = _dot_nt((qh * jnp.exp2(g2)).astype(BF16), st.astype(BF16))

            slot = ci * len(heads) + st_slot
            g2_scr[slot] = g2
            k_scr[slot] = kh
            a_rows = []
            for i in range(n_sub):
                r0 = i * sub
                gi = g2[r0:r0 + sub]
                qi = qh[r0:r0 + sub]
                a_i = jnp.zeros((sub, chunk), F32)
                for s in range(sub):
                    dec = jnp.exp2(gi - g2_scr[slot, pl.ds(r0 + s, 1), :])
                    col = jnp.sum(qi * k_scr[slot, pl.ds(r0 + s, 1), :] * dec, axis=-1, keepdims=True)
                    a_i = jnp.where(diag_code[i] == s, col, a_i)
                a_rows.append(a_i)
            a = a_rows[0] if n_sub == 1 else jnp.concatenate(a_rows, axis=0)
            for half in halves:
                shift = half.bit_length() - 1
                bounds = [jnp.broadcast_to(g2[r0 + half - 1:r0 + half], (2 * half, B_DIM))
                          for r0 in range(0, chunk, 2 * half)]
                bnd = bounds[0] if len(bounds) == 1 else jnp.concatenate(bounds, axis=0)
                e_half = jnp.exp2(-jnp.abs(g2 - bnd))
                upper = ((row_c >> shift) & 1) == 1
                qo = jnp.where(upper, qh * e_half, 0.0).astype(BF16)
                kk = jnp.where(upper, 0.0, kh * e_half).astype(BF16)
                a_half = _dot_nt(qo, kk)
                if 2 * half < chunk:
                    a_half = jnp.where((t_idx >> (shift + 1)) == (s_idx >> (shift + 1)), a_half, 0.0)
                a = a + a_half
            v16 = v.astype(BF16)
            o = o + _dot(a.astype(BF16), v16)

            g_last = g2[chunk - 1:chunk]
            kd = (kh * jnp.exp2(g_last - g2)).astype(BF16)
            st_ref[st_slot] = jnp.exp2(g_last) * st + _dot_tn(v16, kd)

            o_ref[bi, rows, cols] = _rms(o, g_ref[...]) * (xg * _sigmoid(xg))

    @pl.when(t_id == pl.num_programs(1) - 1)
    def _():
        for bi, h in heads:
            sfin_ref[bi, h] = st_ref[bi * B_HEADS + h].T


def _hgrn(z, lb_logits, norm_g, s0, s0_layer, layer, batch, seq, bb, tt, chunk, sub):
    zv = z.reshape(batch, seq, N_SPLITS * A_WIDTH)
    zspec = lambda col: pl.BlockSpec((bb, tt, B_WIDTH), lambda b, t: (b, t, col))
    sspec = pl.BlockSpec((bb, B_HEADS, B_DIM, B_DIM), lambda b, t: (b, 0, 0, 0))
    s0spec = pl.BlockSpec((None, bb, B_HEADS, B_DIM, B_DIM), lambda b, t: (s0_layer, b, 0, 0, 0))
    depth = lb_logits.shape[0]
    o, sfin = pl.pallas_call(
        functools.partial(_hgrn_kernel, layer=layer, chunk=chunk, sub=sub),
        grid=(batch // bb, seq // tt),
        in_specs=[zspec(3), zspec(4), zspec(5), zspec(6),
                  pl.BlockSpec((depth, B_WIDTH), lambda b, t: (0, 0)),
                  pl.BlockSpec((None, 1, B_DIM), lambda b, t: (layer, 0, 0)),
                  s0spec],
        out_specs=[pl.BlockSpec((bb, tt, B_WIDTH), lambda b, t: (b, t, 0)), sspec],
        out_shape=[jax.ShapeDtypeStruct((batch, seq, B_WIDTH), F32),
                   jax.ShapeDtypeStruct((batch, B_HEADS, B_DIM, B_DIM), F32)],
        scratch_shapes=[pltpu.VMEM((bb * B_HEADS, B_DIM, B_DIM), F32)]
        + [pltpu.VMEM((tt // chunk * bb * B_HEADS, chunk, B_DIM), F32)] * 2,
        compiler_params=_cparams(2),
        name="hgrn",
    )(zv, zv, zv, zv, lb_logits, norm_g, s0)
    return o.reshape(batch * seq, B_WIDTH), sfin


def kernel(x_prompt, x_sample, cache_attn_k, cache_attn_v, state_hgrn, ff1_pre_g, ff1_w_gate, ff1_w_up, ff1_w_down, ff1_post_g, mix_pre_g, w_in, attn_norm_g, hgrn_lb_logits, hgrn_norm_g, w_out, mix_post_g, ff2_pre_g, ff2_w_gate, ff2_w_up, ff2_w_down, ff2_post_g):
    batch, seq, d = x_prompt.shape
    dec_batch, dec_seq, _ = x_sample.shape
    depth = w_in.shape[0]
    keep = min(DILATIONS[-1] * WINDOW_STEPS, seq)

    row = lambda g: g.reshape(depth, 1, g.shape[-1])
    bf = lambda w: w.astype(BF16)
    ff1 = (row(ff1_pre_g), bf(ff1_w_gate), bf(ff1_w_up), bf(ff1_w_down), row(ff1_post_g))
    ff2 = (row(ff2_pre_g), bf(ff2_w_gate), bf(ff2_w_up), bf(ff2_w_down), row(ff2_post_g))
    mix_pre_g, attn_norm_g, hgrn_norm_g, mix_post_g = map(row, (mix_pre_g, attn_norm_g, hgrn_norm_g, mix_post_g))
    w_in16, w_out16 = bf(w_in), bf(w_out)
    zero_state = jnp.zeros((1, batch, B_HEADS, B_DIM, B_DIM), F32)

    hp = x_prompt.reshape(batch * seq, d)
    hs = x_sample.reshape(dec_batch * dec_seq, d)
    tm_p, tm_s, tf = 512, dec_batch * dec_seq, 256
    zps = []
    outs = {k: [] for k in ("sp", "ks", "vs", "ss")}
    for l in range(depth):
        hp, zp = _ffn(hp, *ff1, l, tm_p, tf, proj=(mix_pre_g, w_in16))
        att = _attn_prompt(zp, batch, seq)
        ob, sp = _hgrn(zp, hgrn_lb_logits, hgrn_norm_g, zero_state, 0, l, batch, seq,
                       1, 8 * HGRN_CHUNK, HGRN_CHUNK, HGRN_SUB)
        hp = _ffn(hp, *ff2, l, tm_p, tf, mixer=(att, ob, attn_norm_g, w_out16, mix_post_g))
        zps.append(zp)
        outs["sp"].append(sp)

        hs, zs = _ffn(hs, *ff1, l, tm_s, tf, proj=(mix_pre_g, w_in16))
        att_s = _attn_sample(zs, cache_attn_k, cache_attn_v, l, dec_batch, dec_seq)
        ob_s, ss = _hgrn(zs, hgrn_lb_logits, hgrn_norm_g, state_hgrn, l, l, dec_batch, dec_seq,
                         4, dec_seq, dec_seq, dec_seq)
        hs = _ffn(hs, *ff2, l, tm_s, tf, mixer=(att_s, ob_s, attn_norm_g, w_out16, mix_post_g))
        zs3 = zs.reshape(dec_batch, dec_seq, N_SPLITS, A_HEADS, A_HEAD_DIM)
        outs["ks"].append(zs3[:, :, 1])
        outs["vs"].append(zs3[:, :, 2])
        outs["ss"].append(ss)

    kp, vp = _kv_format(zps, batch, seq, keep, 512)
    stack = lambda k: jnp.stack(outs[k])
    return (hp.reshape(batch, seq, d), hs.reshape(dec_batch, dec_seq, d),
            kp, vp, stack("sp"), stack("ks"), stack("vs"), stack("ss"))
```

```python
import functools
import math

import jax
import jax.numpy as jnp
from jax import lax
from jax.experimental import pallas as pl
from jax.experimental.pallas import tpu as pltpu

F32 = jnp.float32
BF16 = jnp.bfloat16

NORM_EPS = 1e-6
LOG2_E = math.log2(math.e)
A_HEADS = 8
A_HEAD_DIM = 64
A_WIDTH = A_HEADS * A_HEAD_DIM
A_PAIR = 2 * A_HEAD_DIM
B_HEADS = 4
B_DIM = 128
B_WIDTH = B_HEADS * B_DIM
DILATIONS = (1, 4, 16)
WINDOW_STEPS = 128
N_SPLITS = 7
HGRN_CHUNK = 64
HGRN_SUB = 8

V7X_VMEM_LIMIT_BYTES = 56 * 1024 * 1024


def _cparams(n_grid_axes, parallel_first=True):
    first = "parallel" if parallel_first else "arbitrary"
    sem = (first,) + ("arbitrary",) * (n_grid_axes - 1)
    return pltpu.CompilerParams(dimension_semantics=sem,
                                vmem_limit_bytes=V7X_VMEM_LIMIT_BYTES)


def _rms(x, g):
    return x * lax.rsqrt(jnp.mean(x * x, axis=-1, keepdims=True) + NORM_EPS) * g


def _sigmoid(x):
    return 1.0 / (1.0 + jnp.exp(-x))


def _dot(a, b):
    return jnp.dot(a, b, preferred_element_type=F32)


def _dot_nt(a, b):
    return lax.dot_general(a, b, (((1,), (1,)), ((), ())), preferred_element_type=F32)


def _dot_tn(a, b):
    return lax.dot_general(a, b, (((0,), (0,)), ((), ())), preferred_element_type=F32)


def _ffn_kernel(*refs, tf, splits, mixer, proj):
    if mixer:
        att_ref, ob_ref, attn_g_ref, wo_ref, mix_post_g_ref = refs[:5]
        refs = refs[5:]
    x_ref, pre_g_ref, wg_ref, wu_ref, wd_ref, post_g_ref = refs[:6]
    refs = refs[6:]
    if proj:
        mix_pre_g_ref, w_in_ref, o_ref, z_ref, xn_ref, acc_ref = refs
    else:
        o_ref, xn_ref, acc_ref = refs
    tm = x_ref.shape[0]
    subs = [slice(i * tm // splits, (i + 1) * tm // splits) for i in range(splits)]
    res_ref = o_ref if mixer else x_ref
    for r in subs:
        if mixer:
            oa = _rms(att_ref[r, :], attn_g_ref[...]).astype(BF16)
            m = _dot(oa, wo_ref[:A_WIDTH, :]) + _dot(ob_ref[r, :].astype(BF16), wo_ref[A_WIDTH:, :])
            o_ref[r, :] = x_ref[r, :] + _rms(m, mix_post_g_ref[...])
        xn_ref[r, :] = _rms(res_ref[r, :], pre_g_ref[...]).astype(BF16)
    ff = wg_ref.shape[-1]
    for c in range(ff // tf):
        cols = slice(c * tf, (c + 1) * tf)
        for r in subs:
            xn = xn_ref[r, :]
            g = _dot(xn, wg_ref[:, cols])
            u = _dot(xn, wu_ref[:, cols])
            a = (g * _sigmoid(g) * u).astype(BF16)
            part = _dot(a, wd_ref[cols, :])
            if c == 0:
                acc_ref[r, :] = part
            else:
                acc_ref[r, :] += part
    for r in subs:
        o_ref[r, :] = res_ref[r, :] + 0.5 * _rms(acc_ref[r, :], post_g_ref[...])
        if proj:
            xn_ref[r, :] = _rms(o_ref[r, :], mix_pre_g_ref[...]).astype(BF16)
            for c in range(N_SPLITS):
                cols = slice(c * A_WIDTH, (c + 1) * A_WIDTH)
                z_ref[r, cols] = _dot(xn_ref[r, :], w_in_ref[:, cols])


def _resident(block_shape, index_map):
    return pl.BlockSpec(block_shape, index_map, pipeline_mode=pl.Buffered(1))


def _ffn(x, pre_g, wg, wu, wd, post_g, layer, tm, tf, splits=1, mixer=None, proj=None):
    m, d = x.shape
    ff = wg.shape[-1]
    gain = lambda width: _resident((None, 1, width), lambda i: (layer, 0, 0))
    rows = lambda width: pl.BlockSpec((tm, width), lambda i: (i, 0))
    weight = lambda k, n: _resident((None, k, n), lambda i: (layer, 0, 0))
    mixer_specs, proj_specs = [], []
    out_specs, out_shape = rows(d), jax.ShapeDtypeStruct((m, d), F32)
    if mixer is not None:
        mixer_specs = [rows(A_WIDTH), rows(B_WIDTH), gain(A_WIDTH), weight(A_WIDTH + B_WIDTH, d), gain(d)]
    if proj is not None:
        n = proj[1].shape[-1]
        proj_specs = [gain(d), weight(d, n)]
        out_specs, out_shape = [out_specs, rows(n)], [out_shape, jax.ShapeDtypeStruct((m, n), F32)]
    return pl.pallas_call(
        functools.partial(_ffn_kernel, tf=tf, splits=splits, mixer=mixer is not None, proj=proj is not None),
        grid=(m // tm,),
        in_specs=mixer_specs + [rows(d), gain(d), weight(d, ff), weight(d, ff), weight(ff, d), gain(d)] + proj_specs,
        out_specs=out_specs,
        out_shape=out_shape,
        scratch_shapes=[pltpu.VMEM((tm, d), BF16), pltpu.VMEM((tm, d), F32)],
        compiler_params=_cparams(1),
        name=("mix_out_" if mixer is not None else "") + "ffn" + ("_mix_in" if proj is not None else ""),
    )(*(mixer or ()), x, pre_g, wg, wu, wd, post_g, *(proj or ()))


def _mix_in_kernel(h_ref, g_ref, w_ref, z_ref, hn_ref, *, splits):
    tm = h_ref.shape[0]
    subs = [slice(i * tm // splits, (i + 1) * tm // splits) for i in range(splits)]
    for r in subs:
        hn_ref[r, :] = _rms(h_ref[r, :], g_ref[...]).astype(BF16)
    for c in range(N_SPLITS):
        cols = slice(c * A_WIDTH, (c + 1) * A_WIDTH)
        for r in subs:
            z_ref[r, cols] = _dot(hn_ref[r, :], w_ref[:, cols])


def _mix_in(h, g, w_in, layer, tm, splits=1):
    m, d = h.shape
    n = w_in.shape[-1]
    return pl.pallas_call(
        functools.partial(_mix_in_kernel, splits=splits),
        grid=(m // tm,),
        in_specs=[
            pl.BlockSpec((tm, d), lambda i: (i, 0)),
            _resident((None, 1, d), lambda i: (layer, 0, 0)),
            _resident((None, d, n), lambda i: (layer, 0, 0)),
        ],
        out_specs=pl.BlockSpec((tm, n), lambda i: (i, 0)),
        out_shape=jax.ShapeDtypeStruct((m, n), F32),
        scratch_shapes=[pltpu.VMEM((tm, d), BF16)],
        compiler_params=_cparams(1),
        name="mix_in",
    )(h, g, w_in)


def _attn_block(q, k_prev, k_own, v_prev, v_own, bias, first_head):
    w = q.shape[0]
    q = q * (LOG2_E / math.sqrt(A_HEAD_DIM))
    q2 = jnp.concatenate([jnp.where(first_head, q, 0.0), jnp.where(first_head, 0.0, q)], axis=0)
    kcat = jnp.concatenate([k_prev, k_own], axis=0).astype(BF16)
    vcat = jnp.concatenate([v_prev, v_own], axis=0).astype(BF16)
    s = _dot_nt(q2.astype(BF16), kcat) + jnp.concatenate([bias, bias], axis=0)
    m_row = jnp.max(s, axis=-1, keepdims=True)
    p = jnp.exp2(s - m_row)
    l_row = jnp.sum(p, axis=-1, keepdims=True)
    pv = _dot(p.astype(BF16), vcat)
    pick = lambda x: jnp.where(first_head, x[:w], x[w:])
    return (pick(jnp.broadcast_to(m_row, (2 * w, A_PAIR))),
            pick(jnp.broadcast_to(l_row, (2 * w, A_PAIR))), pick(pv))


def _attn_merge(m_o, l_o, acc_o, m_b, l_b, acc_b):
    m_n = jnp.maximum(m_o, m_b)
    a_o = jnp.exp2(m_o - m_n)
    a_b = jnp.exp2(m_b - m_n)
    return m_n, a_o * l_o + a_b * l_b, a_o * acc_o + a_b * acc_b


def _attn_prompt_kernel(q_ref, k_ref, v_ref, o_ref, m_ref, l_ref, acc_ref, q4_ref, k4_ref, v4_ref,
                        m4_ref, l4_ref, acc4_ref, bias_ref, *, unroll):
    w = WINDOW_STEPS
    seq = q_ref.shape[0]
    d1, d2 = DILATIONS[1], DILATIONS[2]
    ratio = d2 // d1
    per_res = seq // d1
    nb1 = per_res // w
    nb2 = seq // (d2 * w)
    first_head = lax.broadcasted_iota(jnp.int32, (w, A_PAIR), 1) < A_HEAD_DIM
    qi = lax.broadcasted_iota(jnp.int32, (w, 2 * w), 0)
    ki = lax.broadcasted_iota(jnp.int32, (w, 2 * w), 1)
    dist = w + qi - ki
    band = (dist >= 0) & (dist <= w)
    bias_ref[0] = jnp.where(band & (ki >= w), 0.0, -jnp.inf)
    bias_ref[1] = jnp.where(band, 0.0, -jnp.inf)

    def pattern0(c, carry):
        rows = pl.ds(pl.multiple_of(c * w, w), w)
        prev = pl.ds(pl.multiple_of(jnp.maximum(c - 1, 0) * w, w), w)
        m_b, l_b, acc_b = _attn_block(q_ref[rows, :], k_ref[prev, :], k_ref[rows, :], v_ref[prev, :],
                                      v_ref[rows, :], bias_ref[jnp.minimum(c, 1)], first_head)
        m_ref[rows, :] = m_b
        l_ref[rows, :] = l_b
        acc_ref[rows, :] = acc_b
        return carry

    lax.fori_loop(0, seq // w, pattern0, 0, unroll=unroll)

    for r in range(d1):
        dst = pl.ds(r * per_res, per_res)
        src = pl.ds(r, per_res, stride=d1)
        for nat, grouped in ((q_ref, q4_ref), (k_ref, k4_ref), (v_ref, v4_ref),
                             (m_ref, m4_ref), (l_ref, l4_ref), (acc_ref, acc4_ref)):
            grouped[dst, :] = nat[src, :]

    def pattern1(idx, carry):
        has_prev = jnp.minimum(idx & (nb1 - 1), 1)
        rows = pl.ds(pl.multiple_of(idx * w, w), w)
        prev = pl.ds(pl.multiple_of((idx - has_prev) * w, w), w)
        blk = _attn_block(q4_ref[rows, :], k4_ref[prev, :], k4_ref[rows, :], v4_ref[prev, :],
                          v4_ref[rows, :], bias_ref[has_prev], first_head)
        m_n, l_n, acc_n = _attn_merge(m4_ref[rows, :], l4_ref[rows, :], acc4_ref[rows, :], *blk)
        m4_ref[rows, :] = m_n
        l4_ref[rows, :] = l_n
        acc4_ref[rows, :] = acc_n
        return carry

    assert nb1 & (nb1 - 1) == 0 and nb2 & (nb2 - 1) == 0 and ratio & (ratio - 1) == 0
    lax.fori_loop(0, seq // w, pattern1, 0, unroll=unroll)

    def pattern2(idx, carry):
        a = idx & (ratio - 1)
        c = (idx >> (ratio.bit_length() - 1)) & (nb2 - 1)
        r = idx >> ((ratio * nb2).bit_length() - 1)
        has_prev = jnp.minimum(c, 1)
        start = r * per_res + c * (w * ratio) + a
        rows = pl.ds(start, w, stride=ratio)
        prev = pl.ds(start - has_prev * (w * ratio), w, stride=ratio)
        blk = _attn_block(q4_ref[rows, :], k4_ref[prev, :], k4_ref[rows, :], v4_ref[prev, :],
                          v4_ref[rows, :], bias_ref[has_prev], first_head)
        _, l_n, acc_n = _attn_merge(m4_ref[rows, :], l4_ref[rows, :], acc4_ref[rows, :], *blk)
        acc_ref[rows, :] = acc_n / l_n
        return carry

    lax.fori_loop(0, seq // w, pattern2, 0, unroll=unroll)

    for r in range(d1):
        o_ref[pl.ds(r, per_res, stride=d1), :] = acc_ref[pl.ds(r * per_res, per_res), :]


def _attn_prompt(z, batch, seq, unroll=8):
    assert DILATIONS[0] == 1 and len(DILATIONS) == 3 and DILATIONS[2] % DILATIONS[1] == 0
    assert seq % (DILATIONS[-1] * WINDOW_STEPS) == 0
    zv = z.reshape(batch, seq, N_SPLITS * A_WIDTH)
    n_pairs = A_WIDTH // A_PAIR
    spec = lambda split: pl.BlockSpec((None, seq, A_PAIR), lambda b, p: (b, 0, split * n_pairs + p))
    o = pl.pallas_call(
        functools.partial(_attn_prompt_kernel, unroll=unroll),
        grid=(batch, n_pairs),
        in_specs=[spec(0), spec(1), spec(2)],
        out_specs=pl.BlockSpec((None, seq, A_PAIR), lambda b, p: (b, 0, p)),
        out_shape=jax.ShapeDtypeStruct((batch, seq, A_WIDTH), F32),
        scratch_shapes=[pltpu.VMEM((seq, A_PAIR), F32)] * 9
        + [pltpu.VMEM((2, WINDOW_STEPS, 2 * WINDOW_STEPS), F32)],
        compiler_params=_cparams(2),
        name="attn_prompt",
    )(zv, zv, zv)
    return o.reshape(batch * seq, A_WIDTH)


def _pattern_count(delta):
    cnt = jnp.zeros(delta.shape, F32)
    for dil in DILATIONS:
        hit = (delta >= 0) & ((delta & (dil - 1)) == 0) & (delta <= dil * WINDOW_STEPS)
        cnt = cnt + hit.astype(F32)
    return cnt


def _attn_sample_kernel(q_ref, kn_ref, vn_ref, kt_ref, vt_ref, o_ref):
    t = q_ref.shape[0]
    hq = A_HEADS * t
    n_cache = kt_ref.shape[-1]
    assert t & (t - 1) == 0
    t_shift = t.bit_length() - 1

    def weights(n_cols, first_pos):
        query = lax.broadcasted_iota(jnp.int32, (hq, n_cols), 0) & (t - 1)
        key = first_pos + lax.broadcasted_iota(jnp.int32, (hq, n_cols), 1)
        return _pattern_count(n_cache + query - key)

    w_cache = weights(n_cache, 0)
    w_new = weights(t, n_cache)

    row_head = lax.broadcasted_iota(jnp.int32, (hq, A_WIDTH), 0) >> t_shift
    lane_head = lax.broadcasted_iota(jnp.int32, (hq, A_WIDTH), 1) >> (A_HEAD_DIM.bit_length() - 1)
    own_head = row_head == lane_head
    q = q_ref[...] * (1.0 / math.sqrt(A_HEAD_DIM))
    q_bd = jnp.where(own_head, jnp.concatenate([q] * A_HEADS, axis=0), 0.0).astype(BF16)

    kt = kt_ref[...].reshape(A_WIDTH, n_cache).astype(BF16)
    vt = vt_ref[...].reshape(A_WIDTH, n_cache).astype(BF16)
    s_c = jnp.where(w_cache > 0, _dot(q_bd, kt), -jnp.inf)
    s_n = jnp.where(w_new > 0, _dot_nt(q_bd, kn_ref[...].astype(BF16)), -jnp.inf)
    m = jnp.maximum(jnp.max(s_c, axis=-1, keepdims=True), jnp.max(s_n, axis=-1, keepdims=True))
    p_c = w_cache * jnp.exp(s_c - m)
    p_n = w_new * jnp.exp(s_n - m)
    l = jnp.sum(p_c, axis=-1, keepdims=True) + jnp.sum(p_n, axis=-1, keepdims=True)
    r = (_dot_nt(p_c.astype(BF16), vt) + _dot(p_n.astype(BF16), vn_ref[...].astype(BF16))) / l
    r = jnp.where(own_head, r, 0.0)
    out = r[0:t]
    for h in range(1, A_HEADS):
        out = out + r[h * t:(h + 1) * t]
    o_ref[...] = out


def _kv_format_kernel(*refs, depth):
    k_refs, v_refs = refs[:depth], refs[depth:2 * depth]
    ko_ref, vo_ref = refs[2 * depth:]
    tm = ko_ref.shape[-1]
    for lyr in range(depth):
        @pl.when(pl.program_id(0) == lyr)
        def _(lyr=lyr):
            for src, dst in ((k_refs[lyr], ko_ref), (v_refs[lyr], vo_ref)):
                dst[...] = src[...].T.reshape(A_HEADS, A_HEAD_DIM, tm)


def _kv_format(zs, batch, seq, keep, tm):
    depth = len(zs)
    zvs = [z.reshape(batch, seq, N_SPLITS * A_WIDTH) for z in zs]
    t0, nt = (seq - keep) // tm, keep // tm

    def in_spec(lyr, split):
        def index(l, b, t):
            before, after = l < lyr, l > lyr
            bb = jnp.where(before, 0, jnp.where(after, batch - 1, b))
            tt = jnp.where(before, 0, jnp.where(after, nt - 1, t))
            return (bb, t0 + tt, split)
        return pl.BlockSpec((None, tm, A_WIDTH), index)

    out_spec = pl.BlockSpec((None, None, A_HEADS, A_HEAD_DIM, tm), lambda l, b, t: (l, b, 0, 0, t))
    shape = jax.ShapeDtypeStruct((depth, batch, A_HEADS, A_HEAD_DIM, keep), F32)
    kt, vt = pl.pallas_call(
        functools.partial(_kv_format_kernel, depth=depth),
        grid=(depth, batch, nt),
        in_specs=[in_spec(l, 1) for l in range(depth)] + [in_spec(l, 2) for l in range(depth)],
        out_specs=[out_spec, out_spec],
        out_shape=[shape, shape],
        compiler_params=_cparams(3, parallel_first=False),
        name="kv_format",
    )(*zvs, *zvs)
    return jnp.transpose(kt, (0, 1, 4, 2, 3)), jnp.transpose(vt, (0, 1, 4, 2, 3))


def _hgrn_kernel(*refs, layer, chunk, sub, rider):
    if rider:
        (xq_ref, xf_ref, xi_ref, xg_ref, lbl_ref, g_ref, s0_ref, sq_ref, skn_ref, svn_ref, skt_ref, svt_ref,
         o_ref, sfin_ref, so_ref, st_ref, g2_scr, k_scr) = refs

        @pl.when(pl.program_id(1) >= 0)
        def _():
            _attn_sample_kernel(sq_ref, skn_ref, svn_ref, skt_ref, svt_ref, so_ref)
    else:
        xq_ref, xf_ref, xi_ref, xg_ref, lbl_ref, g_ref, s0_ref, o_ref, sfin_ref, st_ref, g2_scr, k_scr = refs
    t_id = pl.program_id(1)
    n_seqs = xq_ref.shape[0]
    heads = [(bi, h) for bi in range(n_seqs) for h in range(B_HEADS)]

    @pl.when(t_id == 0)
    def _():
        for bi, h in heads:
            st_ref[bi * B_HEADS + h] = s0_ref[bi, h].T

    logits = lbl_ref[...]
    e = jnp.exp(logits - jnp.max(logits, axis=0, keepdims=True))
    soft = e / jnp.sum(e, axis=0, keepdims=True)
    lb_all = soft[0:1] - soft[0:1]
    for l in range(1, layer + 1):
        lb_all = lb_all + soft[l:l + 1]

    n_chunks = xq_ref.shape[1] // chunk
    n_sub = chunk // sub
    t_idx = lax.broadcasted_iota(jnp.int32, (chunk, chunk), 0)
    s_idx = lax.broadcasted_iota(jnp.int32, (chunk, chunk), 1)
    tri = (t_idx >= s_idx).astype(BF16)
    row_c = lax.broadcasted_iota(jnp.int32, (chunk, B_DIM), 0)
    sub_t = lax.broadcasted_iota(jnp.int32, (sub, chunk), 0)
    sub_c = lax.broadcasted_iota(jnp.int32, (sub, chunk), 1)
    diag_code = [jnp.where((sub_c >= i * sub) & (sub_c - i * sub <= sub_t), sub_c - i * sub, -1)
                 for i in range(n_sub)]
    halves = []
    half = chunk // 2
    while half >= sub:
        halves.append(half)
        half //= 2

    for ci in range(n_chunks):
        rows = slice(ci * chunk, (ci + 1) * chunk)
        for bi, h in heads:
            cols = slice(h * B_DIM, (h + 1) * B_DIM)
            xq = xq_ref[bi, rows, cols]
            xg = xg_ref[bi, rows, cols]
            v = xi_ref[bi, rows, cols]
            lb = lb_all[:, cols]
            qh = xq * _sigmoid(xq)
            f = lb + (1.0 - lb) * _sigmoid(xf_ref[bi, rows, cols])
            logf = jnp.log(f)
            kh = 1.0 - f

            hi = logf.astype(BF16)
            r1 = logf - hi.astype(F32)
            mid = r1.astype(BF16)
            lo = (r1 - mid.astype(F32)).astype(BF16)
            g2 = (_dot(tri, hi) + _dot(tri, mid) + _dot(tri, lo)) * LOG2_E

            st_slot = bi * B_HEADS + h
            st = st_ref[st_slot]
            o = _dot_nt((qh * jnp.exp2(g2)).astype(BF16), st.astype(BF16))

            slot = ci * len(heads) + st_slot
            g2_scr[slot] = g2
            k_scr[slot] = kh
            a_rows = []
            for i in range(n_sub):
                r0 = i * sub
                gi = g2[r0:r0 + sub]
                qi = qh[r0:r0 + sub]
                a_i = jnp.zeros((sub, chunk), F32)
                for s in range(sub):
                    dec = jnp.exp2(gi - g2_scr[slot, pl.ds(r0 + s, 1), :])
                    col = jnp.sum(qi * k_scr[slot, pl.ds(r0 + s, 1), :] * dec, axis=-1, keepdims=True)
                    a_i = jnp.where(diag_code[i] == s, col, a_i)
                a_rows.append(a_i)
            a = a_rows[0] if n_sub == 1 else jnp.concatenate(a_rows, axis=0)
            for half in halves:
                shift = half.bit_length() - 1
                bounds = [jnp.broadcast_to(g2[r0 + half - 1:r0 + half], (2 * half, B_DIM))
                          for r0 in range(0, chunk, 2 * half)]
                bnd = bounds[0] if len(bounds) == 1 else jnp.concatenate(bounds, axis=0)
                e_half = jnp.exp2(-jnp.abs(g2 - bnd))
                upper = ((row_c >> shift) & 1) == 1
                qo = jnp.where(upper, qh * e_half, 0.0).astype(BF16)
                kk = jnp.where(upper, 0.0, kh * e_half).astype(BF16)
                a_half = _dot_nt(qo, kk)
                if 2 * half < chunk:
                    a_half = jnp.where((t_idx >> (shift + 1)) == (s_idx >> (shift + 1)), a_half, 0.0)
                a = a + a_half
            v16 = v.astype(BF16)
            o = o + _dot(a.astype(BF16), v16)

            g_last = g2[chunk - 1:chunk]
            kd = (kh * jnp.exp2(g_last - g2)).astype(BF16)
            st_ref[st_slot] = jnp.exp2(g_last) * st + _dot_tn(v16, kd)

            o_ref[bi, rows, cols] = _rms(o, g_ref[...]) * (xg * _sigmoid(xg))

    @pl.when(t_id == pl.num_programs(1) - 1)
    def _():
        for bi, h in heads:
            sfin_ref[bi, h] = st_ref[bi * B_HEADS + h].T


def _hgrn(z, lb_logits, norm_g, s0, s0_layer, layer, batch, seq, bb, tt, chunk, sub, rider=None):
    zv = z.reshape(batch, seq, N_SPLITS * A_WIDTH)
    n_t = seq // tt
    zspec = lambda col: pl.BlockSpec((bb, tt, B_WIDTH), lambda b, t: (b, t, col))
    sspec = pl.BlockSpec((bb, B_HEADS, B_DIM, B_DIM), lambda b, t: (b, 0, 0, 0))
    s0spec = pl.BlockSpec((None, bb, B_HEADS, B_DIM, B_DIM), lambda b, t: (s0_layer, b, 0, 0, 0))
    depth = lb_logits.shape[0]
    in_specs = [zspec(3), zspec(4), zspec(5), zspec(6),
                pl.BlockSpec((depth, B_WIDTH), lambda b, t: (0, 0)),
                pl.BlockSpec((None, 1, B_DIM), lambda b, t: (layer, 0, 0)),
                s0spec]
    out_specs = [pl.BlockSpec((bb, tt, B_WIDTH), lambda b, t: (b, t, 0)), sspec]
    out_shape = [jax.ShapeDtypeStruct((batch, seq, B_WIDTH), F32),
                 jax.ShapeDtypeStruct((batch, B_HEADS, B_DIM, B_DIM), F32)]
    operands = [zv, zv, zv, zv, lb_logits, norm_g, s0]
    if rider is not None:
        z_s, cache_k, cache_v, s_batch, t_s = rider
        assert s_batch == (batch // bb) * n_t
        n_cache = cache_k.shape[2]
        zs_v = z_s.reshape(s_batch, t_s, N_SPLITS * A_WIDTH)
        ckt = jnp.transpose(cache_k, (0, 1, 3, 4, 2))
        cvt = jnp.transpose(cache_v, (0, 1, 3, 4, 2))
        zs_spec = lambda col: pl.BlockSpec((None, t_s, A_WIDTH), lambda b, t: (b * n_t + t, 0, col))
        cspec = pl.BlockSpec((None, None, A_HEADS, A_HEAD_DIM, n_cache), lambda b, t: (layer, b * n_t + t, 0, 0, 0))
        in_specs += [zs_spec(0), zs_spec(1), zs_spec(2), cspec, cspec]
        out_specs.append(pl.BlockSpec((None, t_s, A_WIDTH), lambda b, t: (b * n_t + t, 0, 0)))
        out_shape.append(jax.ShapeDtypeStruct((s_batch, t_s, A_WIDTH), F32))
        operands += [zs_v, zs_v, zs_v, ckt, cvt]
    res = pl.pallas_call(
        functools.partial(_hgrn_kernel, layer=layer, chunk=chunk, sub=sub, rider=rider is not None),
        grid=(batch // bb, n_t),
        in_specs=in_specs,
        out_specs=out_specs,
        out_shape=out_shape,
        scratch_shapes=[pltpu.VMEM((bb * B_HEADS, B_DIM, B_DIM), F32)]
        + [pltpu.VMEM((tt // chunk * bb * B_HEADS, chunk, B_DIM), F32)] * 2,
        compiler_params=_cparams(2),
        name="hgrn_attn_sample" if rider is not None else "hgrn",
    )(*operands)
    o = res[0].reshape(batch * seq, B_WIDTH)
    if rider is not None:
        return o, res[1], res[2].reshape(s_batch * t_s, A_WIDTH)
    return o, res[1]


def kernel(x_prompt, x_sample, cache_attn_k, cache_attn_v, state_hgrn, ff1_pre_g, ff1_w_gate, ff1_w_up, ff1_w_down, ff1_post_g, mix_pre_g, w_in, attn_norm_g, hgrn_lb_logits, hgrn_norm_g, w_out, mix_post_g, ff2_pre_g, ff2_w_gate, ff2_w_up, ff2_w_down, ff2_post_g):
    batch, seq, d = x_prompt.shape
    dec_batch, dec_seq, _ = x_sample.shape
    depth = w_in.shape[0]
    keep = min(DILATIONS[-1] * WINDOW_STEPS, seq)

    row = lambda g: g.reshape(depth, 1, g.shape[-1])
    bf = lambda w: w.astype(BF16)
    ff1 = (row(ff1_pre_g), bf(ff1_w_gate), bf(ff1_w_up), bf(ff1_w_down), row(ff1_post_g))
    ff2 = (row(ff2_pre_g), bf(ff2_w_gate), bf(ff2_w_up), bf(ff2_w_down), row(ff2_post_g))
    mix_pre_g, attn_norm_g, hgrn_norm_g, mix_post_g = map(row, (mix_pre_g, attn_norm_g, hgrn_norm_g, mix_post_g))
    w_in16, w_out16 = bf(w_in), bf(w_out)
    zero_state = jnp.zeros((1, batch, B_HEADS, B_DIM, B_DIM), F32)

    hp = x_prompt.reshape(batch * seq, d)
    hs = x_sample.reshape(dec_batch * dec_seq, d)
    tm_p, tm_s, tf = 512, dec_batch * dec_seq, 256
    zps = []
    outs = {k: [] for k in ("sp", "ks", "vs", "ss")}
    for l in range(depth):
        hp = _ffn(hp, *ff1, l, 2 * tm_p, tf, splits=2)
        zp = _mix_in(hp, mix_pre_g, w_in16, l, 2 * tm_p, splits=2)
        att = _attn_prompt(zp, batch, seq)
        hs, zs = _ffn(hs, *ff1, l, tm_s, tf, proj=(mix_pre_g, w_in16))
        ob, sp, att_s = _hgrn(zp, hgrn_lb_logits, hgrn_norm_g, zero_state, 0, l, batch, seq,
                              1, 8 * HGRN_CHUNK, HGRN_CHUNK, HGRN_SUB,
                              rider=(zs, cache_attn_k, cache_attn_v, dec_batch, dec_seq))
        hp = _ffn(hp, *ff2, l, 2 * tm_p, tf, splits=2, mixer=(att, ob, attn_norm_g, w_out16, mix_post_g))
        zps.append(zp)
        outs["sp"].append(sp)

        ob_s, ss = _hgrn(zs, hgrn_lb_logits, hgrn_norm_g, state_hgrn, l, l, dec_batch, dec_seq,
                         4, dec_seq, dec_seq, dec_seq)
        hs = _ffn(hs, *ff2, l, tm_s, tf, mixer=(att_s, ob_s, attn_norm_g, w_out16, mix_post_g))
        zs3 = zs.reshape(dec_batch, dec_seq, N_SPLITS, A_HEADS, A_HEAD_DIM)
        outs["ks"].append(zs3[:, :, 1])
        outs["vs"].append(zs3[:, :, 2])
        outs["ss"].append(ss)

    kp, vp = _kv_format(zps, batch, seq, keep, 512)
    stack = lambda k: jnp.stack(outs[k])
    return (hp.reshape(batch, seq, d), hs.reshape(dec_batch, dec_seq, d),
            kp, vp, stack("sp"), stack("ks"), stack("vs"), stack("ss"))
```

```python
import functools
import math

import jax
import jax.numpy as jnp
from jax import lax
from jax.experimental import pallas as pl
from jax.experimental.pallas import tpu as pltpu

F32 = jnp.float32
BF16 = jnp.bfloat16

NORM_EPS = 1e-6
LOG2_E = math.log2(math.e)
A_HEADS = 8
A_HEAD_DIM = 64
A_WIDTH = A_HEADS * A_HEAD_DIM
A_PAIR = 2 * A_HEAD_DIM
B_HEADS = 4
B_DIM = 128
B_WIDTH = B_HEADS * B_DIM
DILATIONS = (1, 4, 16)
WINDOW_STEPS = 128
N_SPLITS = 7
HGRN_CHUNK = 64
HGRN_SUB = 8

V7X_VMEM_LIMIT_BYTES = 56 * 1024 * 1024


def _cparams(n_grid_axes, parallel_first=True):
    first = "parallel" if parallel_first else "arbitrary"
    sem = (first,) + ("arbitrary",) * (n_grid_axes - 1)
    return pltpu.CompilerParams(dimension_semantics=sem,
                                vmem_limit_bytes=V7X_VMEM_LIMIT_BYTES)


def _rms(x, g):
    return x * lax.rsqrt(jnp.mean(x * x, axis=-1, keepdims=True) + NORM_EPS) * g


def _sigmoid(x):
    return 1.0 / (1.0 + jnp.exp(-x))


def _dot(a, b):
    return jnp.dot(a, b, preferred_element_type=F32)


def _dot_nt(a, b):
    return lax.dot_general(a, b, (((1,), (1,)), ((), ())), preferred_element_type=F32)


def _dot_tn(a, b):
    return lax.dot_general(a, b, (((0,), (0,)), ((), ())), preferred_element_type=F32)


def _ffn_kernel(*refs, tf, splits, mixer, proj):
    if mixer:
        att_ref, ob_ref, attn_g_ref, wo_ref, mix_post_g_ref = refs[:5]
        refs = refs[5:]
    x_ref, pre_g_ref, wg_ref, wu_ref, wd_ref, post_g_ref = refs[:6]
    refs = refs[6:]
    if proj:
        mix_pre_g_ref, w_in_ref, o_ref, z_ref, xn_ref, acc_ref = refs
    else:
        o_ref, xn_ref, acc_ref = refs
    tm = x_ref.shape[0]
    subs = [slice(i * tm // splits, (i + 1) * tm // splits) for i in range(splits)]
    res_ref = o_ref if mixer else x_ref
    for r in subs:
        if mixer:
            oa = _rms(att_ref[r, :], attn_g_ref[...]).astype(BF16)
            m = _dot(oa, wo_ref[:A_WIDTH, :]) + _dot(ob_ref[r, :].astype(BF16), wo_ref[A_WIDTH:, :])
            o_ref[r, :] = x_ref[r, :] + _rms(m, mix_post_g_ref[...])
        xn_ref[r, :] = _rms(res_ref[r, :], pre_g_ref[...]).astype(BF16)
    ff = wg_ref.shape[-1]
    for c in range(ff // tf):
        cols = slice(c * tf, (c + 1) * tf)
        for r in subs:
            xn = xn_ref[r, :]
            g = _dot(xn, wg_ref[:, cols])
            u = _dot(xn, wu_ref[:, cols])
            a = (g * _sigmoid(g) * u).astype(BF16)
            part = _dot(a, wd_ref[cols, :])
            if c == 0:
                acc_ref[r, :] = part
            else:
                acc_ref[r, :] += part
    for r in subs:
        o_ref[r, :] = res_ref[r, :] + 0.5 * _rms(acc_ref[r, :], post_g_ref[...])
        if proj:
            xn_ref[r, :] = _rms(o_ref[r, :], mix_pre_g_ref[...]).astype(BF16)
            for c in range(N_SPLITS):
                cols = slice(c * A_WIDTH, (c + 1) * A_WIDTH)
                z_ref[r, cols] = _dot(xn_ref[r, :], w_in_ref[:, cols])


def _resident(block_shape, index_map):
    return pl.BlockSpec(block_shape, index_map, pipeline_mode=pl.Buffered(1))


def _ffn(x, pre_g, wg, wu, wd, post_g, layer, tm, tf, splits=1, mixer=None, proj=None):
    m, d = x.shape
    ff = wg.shape[-1]
    gain = lambda width: _resident((None, 1, width), lambda i: (layer, 0, 0))
    rows = lambda width: pl.BlockSpec((tm, width), lambda i: (i, 0))
    weight = lambda k, n: _resident((None, k, n), lambda i: (layer, 0, 0))
    mixer_specs, proj_specs = [], []
    out_specs, out_shape = rows(d), jax.ShapeDtypeStruct((m, d), F32)
    if mixer is not None:
        mixer_specs = [rows(A_WIDTH), rows(B_WIDTH), gain(A_WIDTH), weight(A_WIDTH + B_WIDTH, d), gain(d)]
    if proj is not None:
        n = proj[1].shape[-1]
        proj_specs = [gain(d), weight(d, n)]
        out_specs, out_shape = [out_specs, rows(n)], [out_shape, jax.ShapeDtypeStruct((m, n), F32)]
    return pl.pallas_call(
        functools.partial(_ffn_kernel, tf=tf, splits=splits, mixer=mixer is not None, proj=proj is not None),
        grid=(m // tm,),
        in_specs=mixer_specs + [rows(d), gain(d), weight(d, ff), weight(d, ff), weight(ff, d), gain(d)] + proj_specs,
        out_specs=out_specs,
        out_shape=out_shape,
        scratch_shapes=[pltpu.VMEM((tm, d), BF16), pltpu.VMEM((tm, d), F32)],
        compiler_params=_cparams(1),
        name=("mix_out_" if mixer is not None else "") + "ffn" + ("_mix_in" if proj is not None else ""),
    )(*(mixer or ()), x, pre_g, wg, wu, wd, post_g, *(proj or ()))


def _mix_in_kernel(h_ref, g_ref, w_ref, z_ref, hn_ref, *, splits):
    tm = h_ref.shape[0]
    subs = [slice(i * tm // splits, (i + 1) * tm // splits) for i in range(splits)]
    for r in subs:
        hn_ref[r, :] = _rms(h_ref[r, :], g_ref[...]).astype(BF16)
    for c in range(N_SPLITS):
        cols = slice(c * A_WIDTH, (c + 1) * A_WIDTH)
        for r in subs:
            z_ref[r, cols] = _dot(hn_ref[r, :], w_ref[:, cols])


def _mix_in(h, g, w_in, layer, tm, splits=1):
    m, d = h.shape
    n = w_in.shape[-1]
    return pl.pallas_call(
        functools.partial(_mix_in_kernel, splits=splits),
        grid=(m // tm,),
        in_specs=[
            pl.BlockSpec((tm, d), lambda i: (i, 0)),
            _resident((None, 1, d), lambda i: (layer, 0, 0)),
            _resident((None, d, n), lambda i: (layer, 0, 0)),
        ],
        out_specs=pl.BlockSpec((tm, n), lambda i: (i, 0)),
        out_shape=jax.ShapeDtypeStruct((m, n), F32),
        scratch_shapes=[pltpu.VMEM((tm, d), BF16)],
        compiler_params=_cparams(1),
        name="mix_in",
    )(h, g, w_in)


def _attn_block(q, k_prev, k_own, v_prev, v_own, bias, first_head):
    w = q.shape[0]
    q = q * (LOG2_E / math.sqrt(A_HEAD_DIM))
    q2 = jnp.concatenate([jnp.where(first_head, q, 0.0), jnp.where(first_head, 0.0, q)], axis=0)
    kcat = jnp.concatenate([k_prev, k_own], axis=0).astype(BF16)
    vcat = jnp.concatenate([v_prev, v_own], axis=0).astype(BF16)
    s = _dot_nt(q2.astype(BF16), kcat) + jnp.concatenate([bias, bias], axis=0)
    m_row = jnp.max(s, axis=-1, keepdims=True)
    p = jnp.exp2(s - m_row)
    l_row = jnp.sum(p, axis=-1, keepdims=True)
    pv = _dot(p.astype(BF16), vcat)
    pick = lambda x: jnp.where(first_head, x[:w], x[w:])
    return (pick(jnp.broadcast_to(m_row, (2 * w, A_PAIR))),
            pick(jnp.broadcast_to(l_row, (2 * w, A_PAIR))), pick(pv))


def _attn_merge(m_o, l_o, acc_o, m_b, l_b, acc_b):
    m_n = jnp.maximum(m_o, m_b)
    a_o = jnp.exp2(m_o - m_n)
    a_b = jnp.exp2(m_b - m_n)
    return m_n, a_o * l_o + a_b * l_b, a_o * acc_o + a_b * acc_b


def _attn_prompt_kernel(q_ref, k_ref, v_ref, o_ref, m_ref, l_ref, acc_ref, q4_ref, k4_ref, v4_ref,
                        m4_ref, l4_ref, acc4_ref, bias_ref, *, unroll):
    w = WINDOW_STEPS
    seq = q_ref.shape[0]
    d1, d2 = DILATIONS[1], DILATIONS[2]
    ratio = d2 // d1
    per_res = seq // d1
    nb1 = per_res // w
    nb2 = seq // (d2 * w)
    first_head = lax.broadcasted_iota(jnp.int32, (w, A_PAIR), 1) < A_HEAD_DIM
    qi = lax.broadcasted_iota(jnp.int32, (w, 2 * w), 0)
    ki = lax.broadcasted_iota(jnp.int32, (w, 2 * w), 1)
    dist = w + qi - ki
    band = (dist >= 0) & (dist <= w)
    bias_ref[0] = jnp.where(band & (ki >= w), 0.0, -jnp.inf)
    bias_ref[1] = jnp.where(band, 0.0, -jnp.inf)

    def pattern0(c, carry):
        rows = pl.ds(pl.multiple_of(c * w, w), w)
        prev = pl.ds(pl.multiple_of(jnp.maximum(c - 1, 0) * w, w), w)
        m_b, l_b, acc_b = _attn_block(q_ref[rows, :], k_ref[prev, :], k_ref[rows, :], v_ref[prev, :],
                                      v_ref[rows, :], bias_ref[jnp.minimum(c, 1)], first_head)
        m_ref[rows, :] = m_b
        l_ref[rows, :] = l_b
        acc_ref[rows, :] = acc_b
        return carry

    lax.fori_loop(0, seq // w, pattern0, 0, unroll=2 * unroll)

    for r in range(d1):
        dst = pl.ds(r * per_res, per_res)
        src = pl.ds(r, per_res, stride=d1)
        for nat, grouped in ((q_ref, q4_ref), (k_ref, k4_ref), (v_ref, v4_ref),
                             (m_ref, m4_ref), (l_ref, l4_ref), (acc_ref, acc4_ref)):
            grouped[dst, :] = nat[src, :]

    def pattern1(idx, carry):
        has_prev = jnp.minimum(idx & (nb1 - 1), 1)
        rows = pl.ds(pl.multiple_of(idx * w, w), w)
        prev = pl.ds(pl.multiple_of((idx - has_prev) * w, w), w)
        blk = _attn_block(q4_ref[rows, :], k4_ref[prev, :], k4_ref[rows, :], v4_ref[prev, :],
                          v4_ref[rows, :], bias_ref[has_prev], first_head)
        m_n, l_n, acc_n = _attn_merge(m4_ref[rows, :], l4_ref[rows, :], acc4_ref[rows, :], *blk)
        m4_ref[rows, :] = m_n
        l4_ref[rows, :] = l_n
        acc4_ref[rows, :] = acc_n
        return carry

    assert nb1 & (nb1 - 1) == 0 and nb2 & (nb2 - 1) == 0 and ratio & (ratio - 1) == 0
    lax.fori_loop(0, seq // w, pattern1, 0, unroll=2 * unroll)

    def pattern2(idx, carry):
        a = idx & (ratio - 1)
        c = (idx >> (ratio.bit_length() - 1)) & (nb2 - 1)
        r = idx >> ((ratio * nb2).bit_length() - 1)
        has_prev = jnp.minimum(c, 1)
        start = r * per_res + c * (w * ratio) + a
        rows = pl.ds(start, w, stride=ratio)
        prev = pl.ds(start - has_prev * (w * ratio), w, stride=ratio)
        blk = _attn_block(q4_ref[rows, :], k4_ref[prev, :], k4_ref[rows, :], v4_ref[prev, :],
                          v4_ref[rows, :], bias_ref[has_prev], first_head)
        _, l_n, acc_n = _attn_merge(m4_ref[rows, :], l4_ref[rows, :], acc4_ref[rows, :], *blk)
        acc_ref[rows, :] = acc_n / l_n
        return carry

    lax.fori_loop(0, seq // w, pattern2, 0, unroll=unroll)

    for r in range(d1):
        o_ref[pl.ds(r, per_res, stride=d1), :] = acc_ref[pl.ds(r * per_res, per_res), :]


def _attn_prompt(z, batch, seq, unroll=8):
    assert DILATIONS[0] == 1 and len(DILATIONS) == 3 and DILATIONS[2] % DILATIONS[1] == 0
    assert seq % (DILATIONS[-1] * WINDOW_STEPS) == 0
    zv = z.reshape(batch, seq, N_SPLITS * A_WIDTH)
    n_pairs = A_WIDTH // A_PAIR
    spec = lambda split: pl.BlockSpec((None, seq, A_PAIR), lambda b, p: (b, 0, split * n_pairs + p))
    o = pl.pallas_call(
        functools.partial(_attn_prompt_kernel, unroll=unroll),
        grid=(batch, n_pairs),
        in_specs=[spec(0), spec(1), spec(2)],
        out_specs=pl.BlockSpec((None, seq, A_PAIR), lambda b, p: (b, 0, p)),
        out_shape=jax.ShapeDtypeStruct((batch, seq, A_WIDTH), F32),
        scratch_shapes=[pltpu.VMEM((seq, A_PAIR), F32)] * 9
        + [pltpu.VMEM((2, WINDOW_STEPS, 2 * WINDOW_STEPS), F32)],
        compiler_params=_cparams(2),
        name="attn_prompt",
    )(zv, zv, zv)
    return o.reshape(batch * seq, A_WIDTH)


def _pattern_count(delta):
    cnt = jnp.zeros(delta.shape, F32)
    for dil in DILATIONS:
        hit = (delta >= 0) & ((delta & (dil - 1)) == 0) & (delta <= dil * WINDOW_STEPS)
        cnt = cnt + hit.astype(F32)
    return cnt


def _attn_sample_kernel(q_ref, kn_ref, vn_ref, kt_ref, vt_ref, o_ref):
    t = q_ref.shape[0]
    hq = A_HEADS * t
    n_cache = kt_ref.shape[-1]
    assert t & (t - 1) == 0
    t_shift = t.bit_length() - 1

    def weights(n_cols, first_pos):
        query = lax.broadcasted_iota(jnp.int32, (hq, n_cols), 0) & (t - 1)
        key = first_pos + lax.broadcasted_iota(jnp.int32, (hq, n_cols), 1)
        return _pattern_count(n_cache + query - key)

    w_cache = weights(n_cache, 0)
    w_new = weights(t, n_cache)

    row_head = lax.broadcasted_iota(jnp.int32, (hq, A_WIDTH), 0) >> t_shift
    lane_head = lax.broadcasted_iota(jnp.int32, (hq, A_WIDTH), 1) >> (A_HEAD_DIM.bit_length() - 1)
    own_head = row_head == lane_head
    q = q_ref[...] * (1.0 / math.sqrt(A_HEAD_DIM))
    q_bd = jnp.where(own_head, jnp.concatenate([q] * A_HEADS, axis=0), 0.0).astype(BF16)

    kt = kt_ref[...].reshape(A_WIDTH, n_cache).astype(BF16)
    vt = vt_ref[...].reshape(A_WIDTH, n_cache).astype(BF16)
    s_c = jnp.where(w_cache > 0, _dot(q_bd, kt), -jnp.inf)
    s_n = jnp.where(w_new > 0, _dot_nt(q_bd, kn_ref[...].astype(BF16)), -jnp.inf)
    m = jnp.maximum(jnp.max(s_c, axis=-1, keepdims=True), jnp.max(s_n, axis=-1, keepdims=True))
    p_c = w_cache * jnp.exp(s_c - m)
    p_n = w_new * jnp.exp(s_n - m)
    l = jnp.sum(p_c, axis=-1, keepdims=True) + jnp.sum(p_n, axis=-1, keepdims=True)
    r = (_dot_nt(p_c.astype(BF16), vt) + _dot(p_n.astype(BF16), vn_ref[...].astype(BF16))) / l
    r = jnp.where(own_head, r, 0.0)
    out = r[0:t]
    for h in range(1, A_HEADS):
        out = out + r[h * t:(h + 1) * t]
    o_ref[...] = out


def _kv_format_kernel(*refs, depth):
    k_refs, v_refs = refs[:depth], refs[depth:2 * depth]
    ko_ref, vo_ref = refs[2 * depth:]
    tm = ko_ref.shape[-1]
    for lyr in range(depth):
        @pl.when(pl.program_id(0) == lyr)
        def _(lyr=lyr):
            for src, dst in ((k_refs[lyr], ko_ref), (v_refs[lyr], vo_ref)):
                dst[...] = src[...].T.reshape(A_HEADS, A_HEAD_DIM, tm)


def _kv_format(zs, batch, seq, keep, tm):
    depth = len(zs)
    zvs = [z.reshape(batch, seq, N_SPLITS * A_WIDTH) for z in zs]
    t0, nt = (seq - keep) // tm, keep // tm

    def in_spec(lyr, split):
        def index(l, b, t):
            before, after = l < lyr, l > lyr
            bb = jnp.where(before, 0, jnp.where(after, batch - 1, b))
            tt = jnp.where(before, 0, jnp.where(after, nt - 1, t))
            return (bb, t0 + tt, split)
        return pl.BlockSpec((None, tm, A_WIDTH), index)

    out_spec = pl.BlockSpec((None, None, A_HEADS, A_HEAD_DIM, tm), lambda l, b, t: (l, b, 0, 0, t))
    shape = jax.ShapeDtypeStruct((depth, batch, A_HEADS, A_HEAD_DIM, keep), F32)
    kt, vt = pl.pallas_call(
        functools.partial(_kv_format_kernel, depth=depth),
        grid=(depth, batch, nt),
        in_specs=[in_spec(l, 1) for l in range(depth)] + [in_spec(l, 2) for l in range(depth)],
        out_specs=[out_spec, out_spec],
        out_shape=[shape, shape],
        compiler_params=_cparams(3, parallel_first=False),
        name="kv_format",
    )(*zvs, *zvs)
    return jnp.transpose(kt, (0, 1, 4, 2, 3)), jnp.transpose(vt, (0, 1, 4, 2, 3))


def _hgrn_kernel(*refs, layer, chunk, sub, rider):
    if rider:
        (xq_ref, xf_ref, xi_ref, xg_ref, lbl_ref, g_ref, s0_ref, sq_ref, skn_ref, svn_ref, skt_ref, svt_ref,
         o_ref, sfin_ref, so_ref, st_ref, g2_scr, k_scr) = refs

        @pl.when(pl.program_id(1) >= 0)
        def _():
            _attn_sample_kernel(sq_ref, skn_ref, svn_ref, skt_ref, svt_ref, so_ref)
    else:
        xq_ref, xf_ref, xi_ref, xg_ref, lbl_ref, g_ref, s0_ref, o_ref, sfin_ref, st_ref, g2_scr, k_scr = refs
    t_id = pl.program_id(1)
    n_seqs = xq_ref.shape[0]
    heads = [(bi, h) for bi in range(n_seqs) for h in range(B_HEADS)]

    @pl.when(t_id == 0)
    def _():
        for bi, h in heads:
            st_ref[bi * B_HEADS + h] = s0_ref[bi, h].T

    logits = lbl_ref[...]
    e = jnp.exp(logits - jnp.max(logits, axis=0, keepdims=True))
    soft = e / jnp.sum(e, axis=0, keepdims=True)
    lb_all = soft[0:1] - soft[0:1]
    for l in range(1, layer + 1):
        lb_all = lb_all + soft[l:l + 1]

    n_chunks = xq_ref.shape[1] // chunk
    n_sub = chunk // sub
    t_idx = lax.broadcasted_iota(jnp.int32, (chunk, chunk), 0)
    s_idx = lax.broadcasted_iota(jnp.int32, (chunk, chunk), 1)
    tri = (t_idx >= s_idx).astype(BF16)
    row_c = lax.broadcasted_iota(jnp.int32, (chunk, B_DIM), 0)
    sub_t = lax.broadcasted_iota(jnp.int32, (sub, chunk), 0)
    sub_c = lax.broadcasted_iota(jnp.int32, (sub, chunk), 1)
    diag_code = [jnp.where((sub_c >= i * sub) & (sub_c - i * sub <= sub_t), sub_c - i * sub, -1)
                 for i in range(n_sub)]
    halves = []
    half = chunk // 2
    while half >= sub:
        halves.append(half)
        half //= 2

    for ci in range(n_chunks):
        rows = slice(ci * chunk, (ci + 1) * chunk)
        for bi, h in heads:
            cols = slice(h * B_DIM, (h + 1) * B_DIM)
            xq = xq_ref[bi, rows, cols]
            xg = xg_ref[bi, rows, cols]
            v = xi_ref[bi, rows, cols]
            lb = lb_all[:, cols]
            qh = xq * _sigmoid(xq)
            f = lb + (1.0 - lb) * _sigmoid(xf_ref[bi, rows, cols])
            logf = jnp.log(f)
            kh = 1.0 - f

            hi = logf.astype(BF16)
            r1 = logf - hi.astype(F32)
            mid = r1.astype(BF16)
            lo = (r1 - mid.astype(F32)).astype(BF16)
            g2 = (_dot(tri, hi) + _dot(tri, mid) + _dot(tri, lo)) * LOG2_E

            st_slot = bi * B_HEADS + h
            st = st_ref[st_slot]
            o = _dot_nt((qh * jnp.exp2(g2)).astype(BF16), st.astype(BF16))

            slot = ci * len(heads) + st_slot
            g2_scr[slot] = g2
            k_scr[slot] = kh
            a_rows = []
            for i in range(n_sub):
                r0 = i * sub
                gi = g2[r0:r0 + sub]
                qi = qh[r0:r0 + sub]
                a_i = jnp.zeros((sub, chunk), F32)
                for s in range(sub):
                    dec = jnp.exp2(gi - g2_scr[slot, pl.ds(r0 + s, 1), :])
                    col = jnp.sum(qi * k_scr[slot, pl.ds(r0 + s, 1), :] * dec, axis=-1, keepdims=True)
                    a_i = jnp.where(diag_code[i] == s, col, a_i)
                a_rows.append(a_i)
            a = a_rows[0] if n_sub == 1 else jnp.concatenate(a_rows, axis=0)
            for half in halves:
                shift = half.bit_length() - 1
                bounds = [jnp.broadcast_to(g2[r0 + half - 1:r0 + half], (2 * half, B_DIM))
                          for r0 in range(0, chunk, 2 * half)]
                bnd = bounds[0] if len(bounds) == 1 else jnp.concatenate(bounds, axis=0)
                e_half = jnp.exp2(-jnp.abs(g2 - bnd))
                upper = ((row_c >> shift) & 1) == 1
                qo = jnp.where(upper, qh * e_half, 0.0).astype(BF16)
                kk = jnp.where(upper, 0.0, kh * e_half).astype(BF16)
                a_half = _dot_nt(qo, kk)
                if 2 * half < chunk:
                    a_half = jnp.where((t_idx >> (shift + 1)) == (s_idx >> (shift + 1)), a_half, 0.0)
                a = a + a_half
            v16 = v.astype(BF16)
            o = o + _dot(a.astype(BF16), v16)

            g_last = g2[chunk - 1:chunk]
            kd = (kh * jnp.exp2(g_last - g2)).astype(BF16)
            st_ref[st_slot] = jnp.exp2(g_last) * st + _dot_tn(v16, kd)

            o_ref[bi, rows, cols] = _rms(o, g_ref[...]) * (xg * _sigmoid(xg))

    @pl.when(t_id == pl.num_programs(1) - 1)
    def _():
        for bi, h in heads:
            sfin_ref[bi, h] = st_ref[bi * B_HEADS + h].T


def _hgrn(z, lb_logits, norm_g, s0, s0_layer, layer, batch, seq, bb, tt, chunk, sub, rider=None):
    zv = z.reshape(batch, seq, N_SPLITS * A_WIDTH)
    n_t = seq // tt
    zspec = lambda col: pl.BlockSpec((bb, tt, B_WIDTH), lambda b, t: (b, t, col))
    sspec = pl.BlockSpec((bb, B_HEADS, B_DIM, B_DIM), lambda b, t: (b, 0, 0, 0))
    s0spec = pl.BlockSpec((None, bb, B_HEADS, B_DIM, B_DIM), lambda b, t: (s0_layer, b, 0, 0, 0))
    depth = lb_logits.shape[0]
    in_specs = [zspec(3), zspec(4), zspec(5), zspec(6),
                pl.BlockSpec((depth, B_WIDTH), lambda b, t: (0, 0)),
                pl.BlockSpec((None, 1, B_DIM), lambda b, t: (layer, 0, 0)),
                s0spec]
    out_specs = [pl.BlockSpec((bb, tt, B_WIDTH), lambda b, t: (b, t, 0)), sspec]
    out_shape = [jax.ShapeDtypeStruct((batch, seq, B_WIDTH), F32),
                 jax.ShapeDtypeStruct((batch, B_HEADS, B_DIM, B_DIM), F32)]
    operands = [zv, zv, zv, zv, lb_logits, norm_g, s0]
    if rider is not None:
        z_s, cache_k, cache_v, s_batch, t_s = rider
        assert s_batch == (batch // bb) * n_t
        n_cache = cache_k.shape[2]
        zs_v = z_s.reshape(s_batch, t_s, N_SPLITS * A_WIDTH)
        ckt = jnp.transpose(cache_k, (0, 1, 3, 4, 2))
        cvt = jnp.transpose(cache_v, (0, 1, 3, 4, 2))
        zs_spec = lambda col: pl.BlockSpec((None, t_s, A_WIDTH), lambda b, t: (b * n_t + t, 0, col))
        cspec = pl.BlockSpec((None, None, A_HEADS, A_HEAD_DIM, n_cache), lambda b, t: (layer, b * n_t + t, 0, 0, 0))
        in_specs += [zs_spec(0), zs_spec(1), zs_spec(2), cspec, cspec]
        out_specs.append(pl.BlockSpec((None, t_s, A_WIDTH), lambda b, t: (b * n_t + t, 0, 0)))
        out_shape.append(jax.ShapeDtypeStruct((s_batch, t_s, A_WIDTH), F32))
        operands += [zs_v, zs_v, zs_v, ckt, cvt]
    res = pl.pallas_call(
        functools.partial(_hgrn_kernel, layer=layer, chunk=chunk, sub=sub, rider=rider is not None),
        grid=(batch // bb, n_t),
        in_specs=in_specs,
        out_specs=out_specs,
        out_shape=out_shape,
        scratch_shapes=[pltpu.VMEM((bb * B_HEADS, B_DIM, B_DIM), F32)]
        + [pltpu.VMEM((tt // chunk * bb * B_HEADS, chunk, B_DIM), F32)] * 2,
        compiler_params=_cparams(2),
        name="hgrn_attn_sample" if rider is not None else "hgrn",
    )(*operands)
    o = res[0].reshape(batch * seq, B_WIDTH)
    if rider is not None:
        return o, res[1], res[2].reshape(s_batch * t_s, A_WIDTH)
    return o, res[1]


def kernel(x_prompt, x_sample, cache_attn_k, cache_attn_v, state_hgrn, ff1_pre_g, ff1_w_gate, ff1_w_up, ff1_w_down, ff1_post_g, mix_pre_g, w_in, attn_norm_g, hgrn_lb_logits, hgrn_norm_g, w_out, mix_post_g, ff2_pre_g, ff2_w_gate, ff2_w_up, ff2_w_down, ff2_post_g):
    batch, seq, d = x_prompt.shape
    dec_batch, dec_seq, _ = x_sample.shape
    depth = w_in.shape[0]
    keep = min(DILATIONS[-1] * WINDOW_STEPS, seq)

    row = lambda g: g.reshape(depth, 1, g.shape[-1])
    bf = lambda w: w.astype(BF16)
    ff1 = (row(ff1_pre_g), bf(ff1_w_gate), bf(ff1_w_up), bf(ff1_w_down), row(ff1_post_g))
    ff2 = (row(ff2_pre_g), bf(ff2_w_gate), bf(ff2_w_up), bf(ff2_w_down), row(ff2_post_g))
    mix_pre_g, attn_norm_g, hgrn_norm_g, mix_post_g = map(row, (mix_pre_g, attn_norm_g, hgrn_norm_g, mix_post_g))
    w_in16, w_out16 = bf(w_in), bf(w_out)
    zero_state = jnp.zeros((1, batch, B_HEADS, B_DIM, B_DIM), F32)

    hp = x_prompt.reshape(batch * seq, d)
    hs = x_sample.reshape(dec_batch * dec_seq, d)
    tm_p, tm_s, tf = 512, dec_batch * dec_seq, 256
    zps = []
    outs = {k: [] for k in ("sp", "ks", "vs", "ss")}
    for l in range(depth):
        hp = _ffn(hp, *ff1, l, 2 * tm_p, tf, splits=2)
        zp = _mix_in(hp, mix_pre_g, w_in16, l, 2 * tm_p, splits=2)
        att = _attn_prompt(zp, batch, seq)
        hs, zs = _ffn(hs, *ff1, l, tm_s, tf, proj=(mix_pre_g, w_in16))
        ob, sp, att_s = _hgrn(zp, hgrn_lb_logits, hgrn_norm_g, zero_state, 0, l, batch, seq,
                              1, 8 * HGRN_CHUNK, HGRN_CHUNK, HGRN_SUB,
                              rider=(zs, cache_attn_k, cache_attn_v, dec_batch, dec_seq))
        hp = _ffn(hp, *ff2, l, 2 * tm_p, tf, splits=2, mixer=(att, ob, attn_norm_g, w_out16, mix_post_g))
        zps.append(zp)
        outs["sp"].append(sp)

        ob_s, ss = _hgrn(zs, hgrn_lb_logits, hgrn_norm_g, state_hgrn, l, l, dec_batch, dec_seq,
                         4, dec_seq, dec_seq, dec_seq)
        hs = _ffn(hs, *ff2, l, tm_s, tf, mixer=(att_s, ob_s, attn_norm_g, w_out16, mix_post_g))
        zs3 = zs.reshape(dec_batch, dec_seq, N_SPLITS, A_HEADS, A_HEAD_DIM)
        outs["ks"].append(zs3[:, :, 1])
        outs["vs"].append(zs3[:, :, 2])
        outs["ss"].append(ss)

    kp, vp = _kv_format(zps, batch, seq, keep, 512)
    stack = lambda k: jnp.stack(outs[k])
    return (hp.reshape(batch, seq, d), hs.reshape(dec_batch, dec_seq, d),
            kp, vp, stack("sp"), stack("ks"), stack("vs"), stack("ss"))
```

```python
import functools
import math

import jax
import jax.numpy as jnp
from jax import lax
from jax.experimental import pallas as pl
from jax.experimental.pallas import tpu as pltpu

F32 = jnp.float32
BF16 = jnp.bfloat16

NORM_EPS = 1e-6
LOG2_E = math.log2(math.e)
A_HEADS = 8
A_HEAD_DIM = 64
A_WIDTH = A_HEADS * A_HEAD_DIM
A_PAIR = 2 * A_HEAD_DIM
B_HEADS = 4
B_DIM = 128
B_WIDTH = B_HEADS * B_DIM
DILATIONS = (1, 4, 16)
WINDOW_STEPS = 128
N_SPLITS = 7
HGRN_CHUNK = 64
HGRN_SUB = 8

V7X_VMEM_LIMIT_BYTES = 56 * 1024 * 1024


def _cparams(n_grid_axes, parallel_first=True):
    first = "parallel" if parallel_first else "arbitrary"
    sem = (first,) + ("arbitrary",) * (n_grid_axes - 1)
    return pltpu.CompilerParams(dimension_semantics=sem,
                                vmem_limit_bytes=V7X_VMEM_LIMIT_BYTES)


def _rms(x, g):
    return x * lax.rsqrt(jnp.mean(x * x, axis=-1, keepdims=True) + NORM_EPS) * g


def _sigmoid(x):
    return 1.0 / (1.0 + jnp.exp(-x))


def _dot(a, b):
    return jnp.dot(a, b, preferred_element_type=F32)


def _dot_nt(a, b):
    return lax.dot_general(a, b, (((1,), (1,)), ((), ())), preferred_element_type=F32)


def _dot_tn(a, b):
    return lax.dot_general(a, b, (((0,), (0,)), ((), ())), preferred_element_type=F32)


def _ffn_kernel(*refs, tf, splits, mixer, proj):
    if mixer:
        att_ref, ob_ref, attn_g_ref, wo_ref, mix_post_g_ref = refs[:5]
        refs = refs[5:]
    x_ref, pre_g_ref, wg_ref, wu_ref, wd_ref, post_g_ref = refs[:6]
    refs = refs[6:]
    if proj:
        mix_pre_g_ref, w_in_ref, o_ref, z_ref, xn_ref, acc_ref = refs
    else:
        o_ref, xn_ref, acc_ref = refs
    tm = x_ref.shape[0]
    subs = [slice(i * tm // splits, (i + 1) * tm // splits) for i in range(splits)]
    res_ref = o_ref if mixer else x_ref
    for r in subs:
        if mixer:
            oa = _rms(att_ref[r, :], attn_g_ref[...]).astype(BF16)
            m = _dot(oa, wo_ref[:A_WIDTH, :]) + _dot(ob_ref[r, :].astype(BF16), wo_ref[A_WIDTH:, :])
            o_ref[r, :] = x_ref[r, :] + _rms(m, mix_post_g_ref[...])
        xn_ref[r, :] = _rms(res_ref[r, :], pre_g_ref[...]).astype(BF16)
    ff = wg_ref.shape[-1]
    for c in range(ff // tf):
        cols = slice(c * tf, (c + 1) * tf)
        for r in subs:
            xn = xn_ref[r, :]
            g = _dot(xn, wg_ref[:, cols])
            u = _dot(xn, wu_ref[:, cols])
            a = (g * _sigmoid(g) * u).astype(BF16)
            part = _dot(a, wd_ref[cols, :])
            if c == 0:
                acc_ref[r, :] = part
            else:
                acc_ref[r, :] += part
    for r in subs:
        o_ref[r, :] = res_ref[r, :] + 0.5 * _rms(acc_ref[r, :], post_g_ref[...])
        if proj:
            xn_ref[r, :] = _rms(o_ref[r, :], mix_pre_g_ref[...]).astype(BF16)
            for c in range(N_SPLITS):
                cols = slice(c * A_WIDTH, (c + 1) * A_WIDTH)
                z_ref[r, cols] = _dot(xn_ref[r, :], w_in_ref[:, cols])


def _resident(block_shape, index_map):
    return pl.BlockSpec(block_shape, index_map, pipeline_mode=pl.Buffered(1))


def _ffn(x, pre_g, wg, wu, wd, post_g, layer, tm, tf, splits=1, mixer=None, proj=None):
    m, d = x.shape
    ff = wg.shape[-1]
    gain = lambda width: _resident((None, 1, width), lambda i: (layer, 0, 0))
    rows = lambda width: pl.BlockSpec((tm, width), lambda i: (i, 0))
    weight = lambda k, n: _resident((None, k, n), lambda i: (layer, 0, 0))
    mixer_specs, proj_specs = [], []
    out_specs, out_shape = rows(d), jax.ShapeDtypeStruct((m, d), F32)
    if mixer is not None:
        mixer_specs = [rows(A_WIDTH), rows(B_WIDTH), gain(A_WIDTH), weight(A_WIDTH + B_WIDTH, d), gain(d)]
    if proj is not None:
        n = proj[1].shape[-1]
        proj_specs = [gain(d), weight(d, n)]
        out_specs, out_shape = [out_specs, rows(n)], [out_shape, jax.ShapeDtypeStruct((m, n), F32)]
    return pl.pallas_call(
        functools.partial(_ffn_kernel, tf=tf, splits=splits, mixer=mixer is not None, proj=proj is not None),
        grid=(m // tm,),
        in_specs=mixer_specs + [rows(d), gain(d), weight(d, ff), weight(d, ff), weight(ff, d), gain(d)] + proj_specs,
        out_specs=out_specs,
        out_shape=out_shape,
        scratch_shapes=[pltpu.VMEM((tm, d), BF16), pltpu.VMEM((tm, d), F32)],
        compiler_params=_cparams(1),
        name=("mix_out_" if mixer is not None else "") + "ffn" + ("_mix_in" if proj is not None else ""),
    )(*(mixer or ()), x, pre_g, wg, wu, wd, post_g, *(proj or ()))


def _mix_in_kernel(h_ref, g_ref, w_ref, z_ref, hn_ref, *, splits):
    tm = h_ref.shape[0]
    subs = [slice(i * tm // splits, (i + 1) * tm // splits) for i in range(splits)]
    for r in subs:
        hn_ref[r, :] = _rms(h_ref[r, :], g_ref[...]).astype(BF16)
    for c in range(N_SPLITS):
        cols = slice(c * A_WIDTH, (c + 1) * A_WIDTH)
        for r in subs:
            z_ref[r, cols] = _dot(hn_ref[r, :], w_ref[:, cols])


def _mix_in(h, g, w_in, layer, tm, splits=1):
    m, d = h.shape
    n = w_in.shape[-1]
    return pl.pallas_call(
        functools.partial(_mix_in_kernel, splits=splits),
        grid=(m // tm,),
        in_specs=[
            pl.BlockSpec((tm, d), lambda i: (i, 0)),
            _resident((None, 1, d), lambda i: (layer, 0, 0)),
            _resident((None, d, n), lambda i: (layer, 0, 0)),
        ],
        out_specs=pl.BlockSpec((tm, n), lambda i: (i, 0)),
        out_shape=jax.ShapeDtypeStruct((m, n), F32),
        scratch_shapes=[pltpu.VMEM((tm, d), BF16)],
        compiler_params=_cparams(1),
        name="mix_in",
    )(h, g, w_in)


def _attn_block(q, k_prev, k_own, v_prev, v_own, bias, first_head):
    w = q.shape[0]
    q = q * (LOG2_E / math.sqrt(A_HEAD_DIM))
    q2 = jnp.concatenate([jnp.where(first_head, q, 0.0), jnp.where(first_head, 0.0, q)], axis=0)
    kcat = jnp.concatenate([k_prev, k_own], axis=0).astype(BF16)
    vcat = jnp.concatenate([v_prev, v_own], axis=0).astype(BF16)
    s = _dot_nt(q2.astype(BF16), kcat) + jnp.concatenate([bias, bias], axis=0)
    m_row = jnp.max(s, axis=-1, keepdims=True)
    p = jnp.exp2(s - m_row)
    l_row = jnp.sum(p, axis=-1, keepdims=True)
    pv = _dot(p.astype(BF16), vcat)
    pick = lambda x: jnp.where(first_head, x[:w], x[w:])
    return (pick(jnp.broadcast_to(m_row, (2 * w, A_PAIR))),
            pick(jnp.broadcast_to(l_row, (2 * w, A_PAIR))), pick(pv))


def _attn_merge(m_o, l_o, acc_o, m_b, l_b, acc_b):
    m_n = jnp.maximum(m_o, m_b)
    a_o = jnp.exp2(m_o - m_n)
    a_b = jnp.exp2(m_b - m_n)
    return m_n, a_o * l_o + a_b * l_b, a_o * acc_o + a_b * acc_b


def _attn_prompt_kernel(q_ref, k_ref, v_ref, o_ref, m_ref, l_ref, acc_ref, q4_ref, k4_ref, v4_ref,
                        m4_ref, l4_ref, acc4_ref, bias_ref, *, unroll):
    w = WINDOW_STEPS
    seq = q_ref.shape[0]
    d1, d2 = DILATIONS[1], DILATIONS[2]
    ratio = d2 // d1
    per_res = seq // d1
    nb1 = per_res // w
    nb2 = seq // (d2 * w)
    first_head = lax.broadcasted_iota(jnp.int32, (w, A_PAIR), 1) < A_HEAD_DIM
    qi = lax.broadcasted_iota(jnp.int32, (w, 2 * w), 0)
    ki = lax.broadcasted_iota(jnp.int32, (w, 2 * w), 1)
    dist = w + qi - ki
    band = (dist >= 0) & (dist <= w)
    bias_ref[0] = jnp.where(band & (ki >= w), 0.0, -jnp.inf)
    bias_ref[1] = jnp.where(band, 0.0, -jnp.inf)

    def pattern0(c, carry):
        rows = pl.ds(pl.multiple_of(c * w, w), w)
        prev = pl.ds(pl.multiple_of(jnp.maximum(c - 1, 0) * w, w), w)
        m_b, l_b, acc_b = _attn_block(q_ref[rows, :], k_ref[prev, :], k_ref[rows, :], v_ref[prev, :],
                                      v_ref[rows, :], bias_ref[jnp.minimum(c, 1)], first_head)
        m_ref[rows, :] = m_b
        l_ref[rows, :] = l_b
        acc_ref[rows, :] = acc_b
        return carry

    lax.fori_loop(0, seq // w, pattern0, 0, unroll=4 * unroll)

    for r in range(d1):
        dst = pl.ds(r * per_res, per_res)
        src = pl.ds(r, per_res, stride=d1)
        for nat, grouped in ((q_ref, q4_ref), (k_ref, k4_ref), (v_ref, v4_ref),
                             (m_ref, m4_ref), (l_ref, l4_ref), (acc_ref, acc4_ref)):
            grouped[dst, :] = nat[src, :]

    def pattern1(idx, carry):
        has_prev = jnp.minimum(idx & (nb1 - 1), 1)
        rows = pl.ds(pl.multiple_of(idx * w, w), w)
        prev = pl.ds(pl.multiple_of((idx - has_prev) * w, w), w)
        blk = _attn_block(q4_ref[rows, :], k4_ref[prev, :], k4_ref[rows, :], v4_ref[prev, :],
                          v4_ref[rows, :], bias_ref[has_prev], first_head)
        m_n, l_n, acc_n = _attn_merge(m4_ref[rows, :], l4_ref[rows, :], acc4_ref[rows, :], *blk)
        m4_ref[rows, :] = m_n
        l4_ref[rows, :] = l_n
        acc4_ref[rows, :] = acc_n
        return carry

    assert nb1 & (nb1 - 1) == 0 and nb2 & (nb2 - 1) == 0 and ratio & (ratio - 1) == 0
    lax.fori_loop(0, seq // w, pattern1, 0, unroll=4 * unroll)

    def pattern2(idx, carry):
        a = idx & (ratio - 1)
        c = (idx >> (ratio.bit_length() - 1)) & (nb2 - 1)
        r = idx >> ((ratio * nb2).bit_length() - 1)
        has_prev = jnp.minimum(c, 1)
        start = r * per_res + c * (w * ratio) + a
        rows = pl.ds(start, w, stride=ratio)
        prev = pl.ds(start - has_prev * (w * ratio), w, stride=ratio)
        blk = _attn_block(q4_ref[rows, :], k4_ref[prev, :], k4_ref[rows, :], v4_ref[prev, :],
                          v4_ref[rows, :], bias_ref[has_prev], first_head)
        _, l_n, acc_n = _attn_merge(m4_ref[rows, :], l4_ref[rows, :], acc4_ref[rows, :], *blk)
        acc_ref[rows, :] = acc_n / l_n
        return carry

    lax.fori_loop(0, seq // w, pattern2, 0, unroll=unroll)

    for r in range(d1):
        o_ref[pl.ds(r, per_res, stride=d1), :] = acc_ref[pl.ds(r * per_res, per_res), :]


def _attn_prompt(z, batch, seq, unroll=8):
    assert DILATIONS[0] == 1 and len(DILATIONS) == 3 and DILATIONS[2] % DILATIONS[1] == 0
    assert seq % (DILATIONS[-1] * WINDOW_STEPS) == 0
    zv = z.reshape(batch, seq, N_SPLITS * A_WIDTH)
    n_pairs = A_WIDTH // A_PAIR
    spec = lambda split: pl.BlockSpec((None, seq, A_PAIR), lambda b, p: (b, 0, split * n_pairs + p))
    o = pl.pallas_call(
        functools.partial(_attn_prompt_kernel, unroll=unroll),
        grid=(batch, n_pairs),
        in_specs=[spec(0), spec(1), spec(2)],
        out_specs=pl.BlockSpec((None, seq, A_PAIR), lambda b, p: (b, 0, p)),
        out_shape=jax.ShapeDtypeStruct((batch, seq, A_WIDTH), F32),
        scratch_shapes=[pltpu.VMEM((seq, A_PAIR), F32)] * 9
        + [pltpu.VMEM((2, WINDOW_STEPS, 2 * WINDOW_STEPS), F32)],
        compiler_params=_cparams(2),
        name="attn_prompt",
    )(zv, zv, zv)
    return o.reshape(batch * seq, A_WIDTH)


def _pattern_count(delta):
    cnt = jnp.zeros(delta.shape, F32)
    for dil in DILATIONS:
        hit = (delta >= 0) & ((delta & (dil - 1)) == 0) & (delta <= dil * WINDOW_STEPS)
        cnt = cnt + hit.astype(F32)
    return cnt


def _attn_sample_kernel(q_ref, kn_ref, vn_ref, kt_ref, vt_ref, o_ref):
    t = q_ref.shape[0]
    hq = A_HEADS * t
    n_cache = kt_ref.shape[-1]
    assert t & (t - 1) == 0
    t_shift = t.bit_length() - 1

    def weights(n_cols, first_pos):
        query = lax.broadcasted_iota(jnp.int32, (hq, n_cols), 0) & (t - 1)
        key = first_pos + lax.broadcasted_iota(jnp.int32, (hq, n_cols), 1)
        return _pattern_count(n_cache + query - key)

    w_cache = weights(n_cache, 0)
    w_new = weights(t, n_cache)

    row_head = lax.broadcasted_iota(jnp.int32, (hq, A_WIDTH), 0) >> t_shift
    lane_head = lax.broadcasted_iota(jnp.int32, (hq, A_WIDTH), 1) >> (A_HEAD_DIM.bit_length() - 1)
    own_head = row_head == lane_head
    q = q_ref[...] * (1.0 / math.sqrt(A_HEAD_DIM))
    q_bd = jnp.where(own_head, jnp.concatenate([q] * A_HEADS, axis=0), 0.0).astype(BF16)

    kt = kt_ref[...].reshape(A_WIDTH, n_cache).astype(BF16)
    vt = vt_ref[...].reshape(A_WIDTH, n_cache).astype(BF16)
    s_c = jnp.where(w_cache > 0, _dot(q_bd, kt), -jnp.inf)
    s_n = jnp.where(w_new > 0, _dot_nt(q_bd, kn_ref[...].astype(BF16)), -jnp.inf)
    m = jnp.maximum(jnp.max(s_c, axis=-1, keepdims=True), jnp.max(s_n, axis=-1, keepdims=True))
    p_c = w_cache * jnp.exp(s_c - m)
    p_n = w_new * jnp.exp(s_n - m)
    l = jnp.sum(p_c, axis=-1, keepdims=True) + jnp.sum(p_n, axis=-1, keepdims=True)
    r = (_dot_nt(p_c.astype(BF16), vt) + _dot(p_n.astype(BF16), vn_ref[...].astype(BF16))) / l
    r = jnp.where(own_head, r, 0.0)
    out = r[0:t]
    for h in range(1, A_HEADS):
        out = out + r[h * t:(h + 1) * t]
    o_ref[...] = out


def _kv_format_kernel(*refs, depth):
    k_refs, v_refs = refs[:depth], refs[depth:2 * depth]
    ko_ref, vo_ref = refs[2 * depth:]
    tm = ko_ref.shape[-1]
    for lyr in range(depth):
        @pl.when(pl.program_id(0) == lyr)
        def _(lyr=lyr):
            for src, dst in ((k_refs[lyr], ko_ref), (v_refs[lyr], vo_ref)):
                dst[...] = src[...].T.reshape(A_HEADS, A_HEAD_DIM, tm)


def _kv_format(zs, batch, seq, keep, tm):
    depth = len(zs)
    zvs = [z.reshape(batch, seq, N_SPLITS * A_WIDTH) for z in zs]
    t0, nt = (seq - keep) // tm, keep // tm

    def in_spec(lyr, split):
        def index(l, b, t):
            before, after = l < lyr, l > lyr
            bb = jnp.where(before, 0, jnp.where(after, batch - 1, b))
            tt = jnp.where(before, 0, jnp.where(after, nt - 1, t))
            return (bb, t0 + tt, split)
        return pl.BlockSpec((None, tm, A_WIDTH), index)

    out_spec = pl.BlockSpec((None, None, A_HEADS, A_HEAD_DIM, tm), lambda l, b, t: (l, b, 0, 0, t))
    shape = jax.ShapeDtypeStruct((depth, batch, A_HEADS, A_HEAD_DIM, keep), F32)
    kt, vt = pl.pallas_call(
        functools.partial(_kv_format_kernel, depth=depth),
        grid=(depth, batch, nt),
        in_specs=[in_spec(l, 1) for l in range(depth)] + [in_spec(l, 2) for l in range(depth)],
        out_specs=[out_spec, out_spec],
        out_shape=[shape, shape],
        compiler_params=_cparams(3, parallel_first=False),
        name="kv_format",
    )(*zvs, *zvs)
    return jnp.transpose(kt, (0, 1, 4, 2, 3)), jnp.transpose(vt, (0, 1, 4, 2, 3))


def _hgrn_kernel(*refs, layer, chunk, sub, rider):
    if rider:
        (xq_ref, xf_ref, xi_ref, xg_ref, lbl_ref, g_ref, s0_ref, sq_ref, skn_ref, svn_ref, skt_ref, svt_ref,
         o_ref, sfin_ref, so_ref, st_ref, g2_scr, k_scr) = refs

        @pl.when(pl.program_id(1) >= 0)
        def _():
            _attn_sample_kernel(sq_ref, skn_ref, svn_ref, skt_ref, svt_ref, so_ref)
    else:
        xq_ref, xf_ref, xi_ref, xg_ref, lbl_ref, g_ref, s0_ref, o_ref, sfin_ref, st_ref, g2_scr, k_scr = refs
    t_id = pl.program_id(1)
    n_seqs = xq_ref.shape[0]
    heads = [(bi, h) for bi in range(n_seqs) for h in range(B_HEADS)]

    @pl.when(t_id == 0)
    def _():
        for bi, h in heads:
            st_ref[bi * B_HEADS + h] = s0_ref[bi, h].T

    logits = lbl_ref[...]
    e = jnp.exp(logits - jnp.max(logits, axis=0, keepdims=True))
    soft = e / jnp.sum(e, axis=0, keepdims=True)
    lb_all = soft[0:1] - soft[0:1]
    for l in range(1, layer + 1):
        lb_all = lb_all + soft[l:l + 1]

    n_chunks = xq_ref.shape[1] // chunk
    n_sub = chunk // sub
    t_idx = lax.broadcasted_iota(jnp.int32, (chunk, chunk), 0)
    s_idx = lax.broadcasted_iota(jnp.int32, (chunk, chunk), 1)
    tri = (t_idx >= s_idx).astype(BF16)
    row_c = lax.broadcasted_iota(jnp.int32, (chunk, B_DIM), 0)
    sub_t = lax.broadcasted_iota(jnp.int32, (sub, chunk), 0)
    sub_c = lax.broadcasted_iota(jnp.int32, (sub, chunk), 1)
    diag_code = [jnp.where((sub_c >= i * sub) & (sub_c - i * sub <= sub_t), sub_c - i * sub, -1)
                 for i in range(n_sub)]
    halves = []
    half = chunk // 2
    while half >= sub:
        halves.append(half)
        half //= 2

    for ci in range(n_chunks):
        rows = slice(ci * chunk, (ci + 1) * chunk)
        for bi, h in heads:
            cols = slice(h * B_DIM, (h + 1) * B_DIM)
            xq = xq_ref[bi, rows, cols]
            xg = xg_ref[bi, rows, cols]
            v = xi_ref[bi, rows, cols]
            lb = lb_all[:, cols]
            qh = xq * _sigmoid(xq)
            f = lb + (1.0 - lb) * _sigmoid(xf_ref[bi, rows, cols])
            logf = jnp.log(f)
            kh = 1.0 - f

            hi = logf.astype(BF16)
            r1 = logf - hi.astype(F32)
            mid = r1.astype(BF16)
            lo = (r1 - mid.astype(F32)).astype(BF16)
            g2 = (_dot(tri, hi) + _dot(tri, mid) + _dot(tri, lo)) * LOG2_E

            st_slot = bi * B_HEADS + h
            st = st_ref[st_slot]
            o = _dot_nt((qh * jnp.exp2(g2)).astype(BF16), st.astype(BF16))

            slot = ci * len(heads) + st_slot
            g2_scr[slot] = g2
            k_scr[slot] = kh
            a_rows = []
            for i in range(n_sub):
                r0 = i * sub
                gi = g2[r0:r0 + sub]
                qi = qh[r0:r0 + sub]
                a_i = jnp.zeros((sub, chunk), F32)
                for s in range(sub):
                    dec = jnp.exp2(gi - g2_scr[slot, pl.ds(r0 + s, 1), :])
                    col = jnp.sum(qi * k_scr[slot, pl.ds(r0 + s, 1), :] * dec, axis=-1, keepdims=True)
                    a_i = jnp.where(diag_code[i] == s, col, a_i)
                a_rows.append(a_i)
            a = a_rows[0] if n_sub == 1 else jnp.concatenate(a_rows, axis=0)
            for half in halves:
                shift = half.bit_length() - 1
                bounds = [jnp.broadcast_to(g2[r0 + half - 1:r0 + half], (2 * half, B_DIM))
                          for r0 in range(0, chunk, 2 * half)]
                bnd = bounds[0] if len(bounds) == 1 else jnp.concatenate(bounds, axis=0)
                e_half = jnp.exp2(-jnp.abs(g2 - bnd))
                upper = ((row_c >> shift) & 1) == 1
                qo = jnp.where(upper, qh * e_half, 0.0).astype(BF16)
                kk = jnp.where(upper, 0.0, kh * e_half).astype(BF16)
                a_half = _dot_nt(qo, kk)
                if 2 * half < chunk:
                    a_half = jnp.where((t_idx >> (shift + 1)) == (s_idx >> (shift + 1)), a_half, 0.0)
                a = a + a_half
            v16 = v.astype(BF16)
            o = o + _dot(a.astype(BF16), v16)

            g_last = g2[chunk - 1:chunk]
            kd = (kh * jnp.exp2(g_last - g2)).astype(BF16)
            st_ref[st_slot] = jnp.exp2(g_last) * st + _dot_tn(v16, kd)

            o_ref[bi, rows, cols] = _rms(o, g_ref[...]) * (xg * _sigmoid(xg))

    @pl.when(t_id == pl.num_programs(1) - 1)
    def _():
        for bi, h in heads:
            sfin_ref[bi, h] = st_ref[bi * B_HEADS + h].T


def _hgrn(z, lb_logits, norm_g, s0, s0_layer, layer, batch, seq, bb, tt, chunk, sub, rider=None):
    zv = z.reshape(batch, seq, N_SPLITS * A_WIDTH)
    n_t = seq // tt
    zspec = lambda col: pl.BlockSpec((bb, tt, B_WIDTH), lambda b, t: (b, t, col))
    sspec = pl.BlockSpec((bb, B_HEADS, B_DIM, B_DIM), lambda b, t: (b, 0, 0, 0))
    s0spec = pl.BlockSpec((None, bb, B_HEADS, B_DIM, B_DIM), lambda b, t: (s0_layer, b, 0, 0, 0))
    depth = lb_logits.shape[0]
    in_specs = [zspec(3), zspec(4), zspec(5), zspec(6),
                pl.BlockSpec((depth, B_WIDTH), lambda b, t: (0, 0)),
                pl.BlockSpec((None, 1, B_DIM), lambda b, t: (layer, 0, 0)),
                s0spec]
    out_specs = [pl.BlockSpec((bb, tt, B_WIDTH), lambda b, t: (b, t, 0)), sspec]
    out_shape = [jax.ShapeDtypeStruct((batch, seq, B_WIDTH), F32),
                 jax.ShapeDtypeStruct((batch, B_HEADS, B_DIM, B_DIM), F32)]
    operands = [zv, zv, zv, zv, lb_logits, norm_g, s0]
    if rider is not None:
        z_s, cache_k, cache_v, s_batch, t_s = rider
        assert s_batch == (batch // bb) * n_t
        n_cache = cache_k.shape[2]
        zs_v = z_s.reshape(s_batch, t_s, N_SPLITS * A_WIDTH)
        ckt = jnp.transpose(cache_k, (0, 1, 3, 4, 2))
        cvt = jnp.transpose(cache_v, (0, 1, 3, 4, 2))
        zs_spec = lambda col: pl.BlockSpec((None, t_s, A_WIDTH), lambda b, t: (b * n_t + t, 0, col))
        cspec = pl.BlockSpec((None, None, A_HEADS, A_HEAD_DIM, n_cache), lambda b, t: (layer, b * n_t + t, 0, 0, 0))
        in_specs += [zs_spec(0), zs_spec(1), zs_spec(2), cspec, cspec]
        out_specs.append(pl.BlockSpec((None, t_s, A_WIDTH), lambda b, t: (b * n_t + t, 0, 0)))
        out_shape.append(jax.ShapeDtypeStruct((s_batch, t_s, A_WIDTH), F32))
        operands += [zs_v, zs_v, zs_v, ckt, cvt]
    res = pl.pallas_call(
        functools.partial(_hgrn_kernel, layer=layer, chunk=chunk, sub=sub, rider=rider is not None),
        grid=(batch // bb, n_t),
        in_specs=in_specs,
        out_specs=out_specs,
        out_shape=out_shape,
        scratch_shapes=[pltpu.VMEM((bb * B_HEADS, B_DIM, B_DIM), F32)]
        + [pltpu.VMEM((tt // chunk * bb * B_HEADS, chunk, B_DIM), F32)] * 2,
        compiler_params=_cparams(2),
        name="hgrn_attn_sample" if rider is not None else "hgrn",
    )(*operands)
    o = res[0].reshape(batch * seq, B_WIDTH)
    if rider is not None:
        return o, res[1], res[2].reshape(s_batch * t_s, A_WIDTH)
    return o, res[1]


def kernel(x_prompt, x_sample, cache_attn_k, cache_attn_v, state_hgrn, ff1_pre_g, ff1_w_gate, ff1_w_up, ff1_w_down, ff1_post_g, mix_pre_g, w_in, attn_norm_g, hgrn_lb_logits, hgrn_norm_g, w_out, mix_post_g, ff2_pre_g, ff2_w_gate, ff2_w_up, ff2_w_down, ff2_post_g):
    batch, seq, d = x_prompt.shape
    dec_batch, dec_seq, _ = x_sample.shape
    depth = w_in.shape[0]
    keep = min(DILATIONS[-1] * WINDOW_STEPS, seq)

    row = lambda g: g.reshape(depth, 1, g.shape[-1])
    bf = lambda w: w.astype(BF16)
    ff1 = (row(ff1_pre_g), bf(ff1_w_gate), bf(ff1_w_up), bf(ff1_w_down), row(ff1_post_g))
    ff2 = (row(ff2_pre_g), bf(ff2_w_gate), bf(ff2_w_up), bf(ff2_w_down), row(ff2_post_g))
    mix_pre_g, attn_norm_g, hgrn_norm_g, mix_post_g = map(row, (mix_pre_g, attn_norm_g, hgrn_norm_g, mix_post_g))
    w_in16, w_out16 = bf(w_in), bf(w_out)
    zero_state = jnp.zeros((1, batch, B_HEADS, B_DIM, B_DIM), F32)

    hp = x_prompt.reshape(batch * seq, d)
    hs = x_sample.reshape(dec_batch * dec_seq, d)
    tm_p, tm_s, tf = 512, dec_batch * dec_seq, 256
    zps = []
    outs = {k: [] for k in ("sp", "ks", "vs", "ss")}
    for l in range(depth):
        hp = _ffn(hp, *ff1, l, 2 * tm_p, tf, splits=2)
        zp = _mix_in(hp, mix_pre_g, w_in16, l, 2 * tm_p, splits=2)
        att = _attn_prompt(zp, batch, seq)
        hs, zs = _ffn(hs, *ff1, l, tm_s, tf, proj=(mix_pre_g, w_in16))
        ob, sp, att_s = _hgrn(zp, hgrn_lb_logits, hgrn_norm_g, zero_state, 0, l, batch, seq,
                              1, 8 * HGRN_CHUNK, HGRN_CHUNK, HGRN_SUB,
                              rider=(zs, cache_attn_k, cache_attn_v, dec_batch, dec_seq))
        hp = _ffn(hp, *ff2, l, 2 * tm_p, tf, splits=2, mixer=(att, ob, attn_norm_g, w_out16, mix_post_g))
        zps.append(zp)
        outs["sp"].append(sp)

        ob_s, ss = _hgrn(zs, hgrn_lb_logits, hgrn_norm_g, state_hgrn, l, l, dec_batch, dec_seq,
                         4, dec_seq, dec_seq, dec_seq)
        hs = _ffn(hs, *ff2, l, tm_s, tf, mixer=(att_s, ob_s, attn_norm_g, w_out16, mix_post_g))
        zs3 = zs.reshape(dec_batch, dec_seq, N_SPLITS, A_HEADS, A_HEAD_DIM)
        outs["ks"].append(zs3[:, :, 1])
        outs["vs"].append(zs3[:, :, 2])
        outs["ss"].append(ss)

    kp, vp = _kv_format(zps, batch, seq, keep, 512)
    stack = lambda k: jnp.stack(outs[k])
    return (hp.reshape(batch, seq, d), hs.reshape(dec_batch, dec_seq, d),
            kp, vp, stack("sp"), stack("ks"), stack("vs"), stack("ss"))
```

```python
import functools
import math

import jax
import jax.numpy as jnp
from jax import lax
from jax.experimental import pallas as pl
from jax.experimental.pallas import tpu as pltpu

F32 = jnp.float32
BF16 = jnp.bfloat16

NORM_EPS = 1e-6
LOG2_E = math.log2(math.e)
A_HEADS = 8
A_HEAD_DIM = 64
A_WIDTH = A_HEADS * A_HEAD_DIM
A_PAIR = 2 * A_HEAD_DIM
B_HEADS = 4
B_DIM = 128
B_WIDTH = B_HEADS * B_DIM
DILATIONS = (1, 4, 16)
WINDOW_STEPS = 128
N_SPLITS = 7
HGRN_CHUNK = 64
HGRN_SUB = 8

V7X_VMEM_LIMIT_BYTES = 56 * 1024 * 1024


def _cparams(n_grid_axes, parallel_first=True):
    first = "parallel" if parallel_first else "arbitrary"
    sem = (first,) + ("arbitrary",) * (n_grid_axes - 1)
    return pltpu.CompilerParams(dimension_semantics=sem,
                                vmem_limit_bytes=V7X_VMEM_LIMIT_BYTES)


def _rms(x, g):
    return x * lax.rsqrt(jnp.mean(x * x, axis=-1, keepdims=True) + NORM_EPS) * g


def _sigmoid(x):
    return 1.0 / (1.0 + jnp.exp(-x))


def _silu(x):
    half = 0.5 * x
    return half * jnp.tanh(half) + half


def _dot(a, b):
    return jnp.dot(a, b, preferred_element_type=F32)


def _dot_nt(a, b):
    return lax.dot_general(a, b, (((1,), (1,)), ((), ())), preferred_element_type=F32)


def _dot_tn(a, b):
    return lax.dot_general(a, b, (((0,), (0,)), ((), ())), preferred_element_type=F32)


def _ffn_kernel(*refs, tf, splits, mixer, proj):
    if mixer:
        att_ref, ob_ref, attn_g_ref, wo_ref, mix_post_g_ref = refs[:5]
        refs = refs[5:]
    x_ref, pre_g_ref, wg_ref, wu_ref, wd_ref, post_g_ref = refs[:6]
    refs = refs[6:]
    if proj:
        mix_pre_g_ref, w_in_ref, o_ref, z_ref, xn_ref, acc_ref = refs
    else:
        o_ref, xn_ref, acc_ref = refs
    tm = x_ref.shape[0]
    subs = [slice(i * tm // splits, (i + 1) * tm // splits) for i in range(splits)]
    res_ref = o_ref if mixer else x_ref
    for r in subs:
        if mixer:
            oa = _rms(att_ref[r, :], attn_g_ref[...]).astype(BF16)
            m = _dot(oa, wo_ref[:A_WIDTH, :]) + _dot(ob_ref[r, :].astype(BF16), wo_ref[A_WIDTH:, :])
            o_ref[r, :] = x_ref[r, :] + _rms(m, mix_post_g_ref[...])
        xn_ref[r, :] = _rms(res_ref[r, :], pre_g_ref[...]).astype(BF16)
    ff = wg_ref.shape[-1]
    for c in range(ff // tf):
        cols = slice(c * tf, (c + 1) * tf)
        for r in subs:
            xn = xn_ref[r, :]
            g = _dot(xn, wg_ref[:, cols])
            u = _dot(xn, wu_ref[:, cols])
            a = (_silu(g) * u).astype(BF16)
            part = _dot(a, wd_ref[cols, :])
            if c == 0:
                acc_ref[r, :] = part
            else:
                acc_ref[r, :] += part
    for r in subs:
        o_ref[r, :] = res_ref[r, :] + 0.5 * _rms(acc_ref[r, :], post_g_ref[...])
        if proj:
            xn_ref[r, :] = _rms(o_ref[r, :], mix_pre_g_ref[...]).astype(BF16)
            for c in range(N_SPLITS):
                cols = slice(c * A_WIDTH, (c + 1) * A_WIDTH)
                z_ref[r, cols] = _dot(xn_ref[r, :], w_in_ref[:, cols])


def _resident(block_shape, index_map):
    return pl.BlockSpec(block_shape, index_map, pipeline_mode=pl.Buffered(1))


def _ffn(x, pre_g, wg, wu, wd, post_g, layer, tm, tf, splits=1, mixer=None, proj=None):
    m, d = x.shape
    ff = wg.shape[-1]
    gain = lambda width: _resident((None, 1, width), lambda i: (layer, 0, 0))
    rows = lambda width: pl.BlockSpec((tm, width), lambda i: (i, 0))
    weight = lambda k, n: _resident((None, k, n), lambda i: (layer, 0, 0))
    mixer_specs, proj_specs = [], []
    out_specs, out_shape = rows(d), jax.ShapeDtypeStruct((m, d), F32)
    if mixer is not None:
        mixer_specs = [rows(A_WIDTH), rows(B_WIDTH), gain(A_WIDTH), weight(A_WIDTH + B_WIDTH, d), gain(d)]
    if proj is not None:
        n = proj[1].shape[-1]
        proj_specs = [gain(d), weight(d, n)]
        out_specs, out_shape = [out_specs, rows(n)], [out_shape, jax.ShapeDtypeStruct((m, n), F32)]
    return pl.pallas_call(
        functools.partial(_ffn_kernel, tf=tf, splits=splits, mixer=mixer is not None, proj=proj is not None),
        grid=(m // tm,),
        in_specs=mixer_specs + [rows(d), gain(d), weight(d, ff), weight(d, ff), weight(ff, d), gain(d)] + proj_specs,
        out_specs=out_specs,
        out_shape=out_shape,
        scratch_shapes=[pltpu.VMEM((tm, d), BF16), pltpu.VMEM((tm, d), F32)],
        compiler_params=_cparams(1),
        name=("mix_out_" if mixer is not None else "") + "ffn" + ("_mix_in" if proj is not None else ""),
    )(*(mixer or ()), x, pre_g, wg, wu, wd, post_g, *(proj or ()))


def _mix_in_kernel(h_ref, g_ref, w_ref, z_ref, hn_ref, *, splits):
    tm = h_ref.shape[0]
    subs = [slice(i * tm // splits, (i + 1) * tm // splits) for i in range(splits)]
    for r in subs:
        hn_ref[r, :] = _rms(h_ref[r, :], g_ref[...]).astype(BF16)
    for c in range(N_SPLITS):
        cols = slice(c * A_WIDTH, (c + 1) * A_WIDTH)
        for r in subs:
            z_ref[r, cols] = _dot(hn_ref[r, :], w_ref[:, cols])


def _mix_in(h, g, w_in, layer, tm, splits=1):
    m, d = h.shape
    n = w_in.shape[-1]
    return pl.pallas_call(
        functools.partial(_mix_in_kernel, splits=splits),
        grid=(m // tm,),
        in_specs=[
            pl.BlockSpec((tm, d), lambda i: (i, 0)),
            _resident((None, 1, d), lambda i: (layer, 0, 0)),
            _resident((None, d, n), lambda i: (layer, 0, 0)),
        ],
        out_specs=pl.BlockSpec((tm, n), lambda i: (i, 0)),
        out_shape=jax.ShapeDtypeStruct((m, n), F32),
        scratch_shapes=[pltpu.VMEM((tm, d), BF16)],
        compiler_params=_cparams(1),
        name="mix_in",
    )(h, g, w_in)


def _attn_block(q, k_prev, k_own, v_prev, v_own, bias, first_head):
    w = q.shape[0]
    q = q * (LOG2_E / math.sqrt(A_HEAD_DIM))
    q2 = jnp.concatenate([jnp.where(first_head, q, 0.0), jnp.where(first_head, 0.0, q)], axis=0)
    kcat = jnp.concatenate([k_prev, k_own], axis=0).astype(BF16)
    vcat = jnp.concatenate([v_prev, v_own], axis=0).astype(BF16)
    s = _dot_nt(q2.astype(BF16), kcat) + jnp.concatenate([bias, bias], axis=0)
    m_row = jnp.max(s, axis=-1, keepdims=True)
    p = jnp.exp2(s - m_row)
    l_row = jnp.sum(p, axis=-1, keepdims=True)
    pv = _dot(p.astype(BF16), vcat)
    pick = lambda x: jnp.where(first_head, x[:w], x[w:])
    return (pick(jnp.broadcast_to(m_row, (2 * w, A_PAIR))),
            pick(jnp.broadcast_to(l_row, (2 * w, A_PAIR))), pick(pv))


def _attn_merge(m_o, l_o, acc_o, m_b, l_b, acc_b):
    m_n = jnp.maximum(m_o, m_b)
    a_o = jnp.exp2(m_o - m_n)
    a_b = jnp.exp2(m_b - m_n)
    return m_n, a_o * l_o + a_b * l_b, a_o * acc_o + a_b * acc_b


def _attn_prompt_kernel(q_ref, k_ref, v_ref, o_ref, m_ref, l_ref, acc_ref, q4_ref, k4_ref, v4_ref,
                        m4_ref, l4_ref, acc4_ref, bias_ref, *, unroll):
    w = WINDOW_STEPS
    seq = q_ref.shape[0]
    d1, d2 = DILATIONS[1], DILATIONS[2]
    ratio = d2 // d1
    per_res = seq // d1
    nb1 = per_res // w
    nb2 = seq // (d2 * w)
    first_head = lax.broadcasted_iota(jnp.int32, (w, A_PAIR), 1) < A_HEAD_DIM
    qi = lax.broadcasted_iota(jnp.int32, (w, 2 * w), 0)
    ki = lax.broadcasted_iota(jnp.int32, (w, 2 * w), 1)
    dist = w + qi - ki
    band = (dist >= 0) & (dist <= w)
    bias_ref[0] = jnp.where(band & (ki >= w), 0.0, -jnp.inf)
    bias_ref[1] = jnp.where(band, 0.0, -jnp.inf)

    def pattern0(c, carry):
        rows = pl.ds(pl.multiple_of(c * w, w), w)
        prev = pl.ds(pl.multiple_of(jnp.maximum(c - 1, 0) * w, w), w)
        m_b, l_b, acc_b = _attn_block(q_ref[rows, :], k_ref[prev, :], k_ref[rows, :], v_ref[prev, :],
                                      v_ref[rows, :], bias_ref[jnp.minimum(c, 1)], first_head)
        m_ref[rows, :] = m_b
        l_ref[rows, :] = l_b
        acc_ref[rows, :] = acc_b
        return carry

    lax.fori_loop(0, seq // w, pattern0, 0, unroll=4 * unroll)

    for r in range(d1):
        dst = pl.ds(r * per_res, per_res)
        src = pl.ds(r, per_res, stride=d1)
        for nat, grouped in ((q_ref, q4_ref), (k_ref, k4_ref), (v_ref, v4_ref),
                             (m_ref, m4_ref), (l_ref, l4_ref), (acc_ref, acc4_ref)):
            grouped[dst, :] = nat[src, :]

    def pattern1(idx, carry):
        has_prev = jnp.minimum(idx & (nb1 - 1), 1)
        rows = pl.ds(pl.multiple_of(idx * w, w), w)
        prev = pl.ds(pl.multiple_of((idx - has_prev) * w, w), w)
        blk = _attn_block(q4_ref[rows, :], k4_ref[prev, :], k4_ref[rows, :], v4_ref[prev, :],
                          v4_ref[rows, :], bias_ref[has_prev], first_head)
        m_n, l_n, acc_n = _attn_merge(m4_ref[rows, :], l4_ref[rows, :], acc4_ref[rows, :], *blk)
        m4_ref[rows, :] = m_n
        l4_ref[rows, :] = l_n
        acc4_ref[rows, :] = acc_n
        return carry

    assert nb1 & (nb1 - 1) == 0 and nb2 & (nb2 - 1) == 0 and ratio & (ratio - 1) == 0
    lax.fori_loop(0, seq // w, pattern1, 0, unroll=4 * unroll)

    def pattern2(idx, carry):
        a = idx & (ratio - 1)
        c = (idx >> (ratio.bit_length() - 1)) & (nb2 - 1)
        r = idx >> ((ratio * nb2).bit_length() - 1)
        has_prev = jnp.minimum(c, 1)
        start = r * per_res + c * (w * ratio) + a
        rows = pl.ds(start, w, stride=ratio)
        prev = pl.ds(start - has_prev * (w * ratio), w, stride=ratio)
        blk = _attn_block(q4_ref[rows, :], k4_ref[prev, :], k4_ref[rows, :], v4_ref[prev, :],
                          v4_ref[rows, :], bias_ref[has_prev], first_head)
        _, l_n, acc_n = _attn_merge(m4_ref[rows, :], l4_ref[rows, :], acc4_ref[rows, :], *blk)
        acc_ref[rows, :] = acc_n / l_n
        return carry

    lax.fori_loop(0, seq // w, pattern2, 0, unroll=unroll)

    for r in range(d1):
        o_ref[pl.ds(r, per_res, stride=d1), :] = acc_ref[pl.ds(r * per_res, per_res), :]


def _attn_prompt(z, batch, seq, unroll=8):
    assert DILATIONS[0] == 1 and len(DILATIONS) == 3 and DILATIONS[2] % DILATIONS[1] == 0
    assert seq % (DILATIONS[-1] * WINDOW_STEPS) == 0
    zv = z.reshape(batch, seq, N_SPLITS * A_WIDTH)
    n_pairs = A_WIDTH // A_PAIR
    spec = lambda split: pl.BlockSpec((None, seq, A_PAIR), lambda b, p: (b, 0, split * n_pairs + p))
    o = pl.pallas_call(
        functools.partial(_attn_prompt_kernel, unroll=unroll),
        grid=(batch, n_pairs),
        in_specs=[spec(0), spec(1), spec(2)],
        out_specs=pl.BlockSpec((None, seq, A_PAIR), lambda b, p: (b, 0, p)),
        out_shape=jax.ShapeDtypeStruct((batch, seq, A_WIDTH), F32),
        scratch_shapes=[pltpu.VMEM((seq, A_PAIR), F32)] * 9
        + [pltpu.VMEM((2, WINDOW_STEPS, 2 * WINDOW_STEPS), F32)],
        compiler_params=_cparams(2),
        name="attn_prompt",
    )(zv, zv, zv)
    return o.reshape(batch * seq, A_WIDTH)


def _pattern_count(delta):
    cnt = jnp.zeros(delta.shape, F32)
    for dil in DILATIONS:
        hit = (delta >= 0) & ((delta & (dil - 1)) == 0) & (delta <= dil * WINDOW_STEPS)
        cnt = cnt + hit.astype(F32)
    return cnt


def _attn_sample_weights(t, n_cache, w_cache_ref, w_new_ref):
    hq = A_HEADS * t
    assert t & (t - 1) == 0

    def weights(n_cols, first_pos):
        query = lax.broadcasted_iota(jnp.int32, (hq, n_cols), 0) & (t - 1)
        key = first_pos + lax.broadcasted_iota(jnp.int32, (hq, n_cols), 1)
        return _pattern_count(n_cache + query - key)

    w_cache_ref[...] = weights(n_cache, 0)
    w_new_ref[...] = weights(t, n_cache)


def _attn_sample_kernel(q_ref, kn_ref, vn_ref, kt_ref, vt_ref, o_ref, w_cache_ref, w_new_ref):
    t = q_ref.shape[0]
    hq = A_HEADS * t
    n_cache = kt_ref.shape[-1]
    t_shift = t.bit_length() - 1
    w_cache = w_cache_ref[...]
    w_new = w_new_ref[...]

    row_head = lax.broadcasted_iota(jnp.int32, (hq, A_WIDTH), 0) >> t_shift
    lane_head = lax.broadcasted_iota(jnp.int32, (hq, A_WIDTH), 1) >> (A_HEAD_DIM.bit_length() - 1)
    own_head = row_head == lane_head
    q = q_ref[...] * (1.0 / math.sqrt(A_HEAD_DIM))
    q_bd = jnp.where(own_head, jnp.concatenate([q] * A_HEADS, axis=0), 0.0).astype(BF16)

    kt = kt_ref[...].reshape(A_WIDTH, n_cache).astype(BF16)
    vt = vt_ref[...].reshape(A_WIDTH, n_cache).astype(BF16)
    s_c = jnp.where(w_cache > 0, _dot(q_bd, kt), -jnp.inf)
    s_n = jnp.where(w_new > 0, _dot_nt(q_bd, kn_ref[...].astype(BF16)), -jnp.inf)
    m = jnp.maximum(jnp.max(s_c, axis=-1, keepdims=True), jnp.max(s_n, axis=-1, keepdims=True))
    p_c = w_cache * jnp.exp(s_c - m)
    p_n = w_new * jnp.exp(s_n - m)
    l = jnp.sum(p_c, axis=-1, keepdims=True) + jnp.sum(p_n, axis=-1, keepdims=True)
    r = (_dot_nt(p_c.astype(BF16), vt) + _dot(p_n.astype(BF16), vn_ref[...].astype(BF16))) / l
    r = jnp.where(own_head, r, 0.0)
    out = r[0:t]
    for h in range(1, A_HEADS):
        out = out + r[h * t:(h + 1) * t]
    o_ref[...] = out


def _kv_format_kernel(*refs, depth):
    k_refs, v_refs = refs[:depth], refs[depth:2 * depth]
    ko_ref, vo_ref = refs[2 * depth:]
    tm = ko_ref.shape[-1]
    for lyr in range(depth):
        @pl.when(pl.program_id(0) == lyr)
        def _(lyr=lyr):
            for src, dst in ((k_refs[lyr], ko_ref), (v_refs[lyr], vo_ref)):
                dst[...] = src[...].T.reshape(A_HEADS, A_HEAD_DIM, tm)


def _kv_format(zs, batch, seq, keep, tm):
    depth = len(zs)
    zvs = [z.reshape(batch, seq, N_SPLITS * A_WIDTH) for z in zs]
    t0, nt = (seq - keep) // tm, keep // tm

    def in_spec(lyr, split):
        def index(l, b, t):
            before, after = l < lyr, l > lyr
            bb = jnp.where(before, 0, jnp.where(after, batch - 1, b))
            tt = jnp.where(before, 0, jnp.where(after, nt - 1, t))
            return (bb, t0 + tt, split)
        return pl.BlockSpec((None, tm, A_WIDTH), index)

    out_spec = pl.BlockSpec((None, None, A_HEADS, A_HEAD_DIM, tm), lambda l, b, t: (l, b, 0, 0, t))
    shape = jax.ShapeDtypeStruct((depth, batch, A_HEADS, A_HEAD_DIM, keep), F32)
    kt, vt = pl.pallas_call(
        functools.partial(_kv_format_kernel, depth=depth),
        grid=(depth, batch, nt),
        in_specs=[in_spec(l, 1) for l in range(depth)] + [in_spec(l, 2) for l in range(depth)],
        out_specs=[out_spec, out_spec],
        out_shape=[shape, shape],
        compiler_params=_cparams(3, parallel_first=False),
        name="kv_format",
    )(*zvs, *zvs)
    return jnp.transpose(kt, (0, 1, 4, 2, 3)), jnp.transpose(vt, (0, 1, 4, 2, 3))


def _hgrn_kernel(*refs, layer, chunk, sub, rider):
    if rider:
        (xq_ref, xf_ref, xi_ref, xg_ref, lbl_ref, g_ref, s0_ref, sq_ref, skn_ref, svn_ref, skt_ref, svt_ref,
         o_ref, sfin_ref, so_ref, st_ref, g2_scr, k_scr, wc_scr, wn_scr) = refs

        @pl.when((pl.program_id(0) == 0) & (pl.program_id(1) == 0))
        def _():
            _attn_sample_weights(sq_ref.shape[0], skt_ref.shape[-1], wc_scr, wn_scr)

        @pl.when(pl.program_id(1) >= 0)
        def _():
            _attn_sample_kernel(sq_ref, skn_ref, svn_ref, skt_ref, svt_ref, so_ref, wc_scr, wn_scr)
    else:
        xq_ref, xf_ref, xi_ref, xg_ref, lbl_ref, g_ref, s0_ref, o_ref, sfin_ref, st_ref, g2_scr, k_scr = refs
    t_id = pl.program_id(1)
    n_seqs = xq_ref.shape[0]
    heads = [(bi, h) for bi in range(n_seqs) for h in range(B_HEADS)]

    @pl.when(t_id == 0)
    def _():
        for bi, h in heads:
            st_ref[bi * B_HEADS + h] = s0_ref[bi, h].T

    logits = lbl_ref[...]
    e = jnp.exp(logits - jnp.max(logits, axis=0, keepdims=True))
    soft = e / jnp.sum(e, axis=0, keepdims=True)
    lb_all = soft[0:1] - soft[0:1]
    for l in range(1, layer + 1):
        lb_all = lb_all + soft[l:l + 1]

    n_chunks = xq_ref.shape[1] // chunk
    n_sub = chunk // sub
    t_idx = lax.broadcasted_iota(jnp.int32, (chunk, chunk), 0)
    s_idx = lax.broadcasted_iota(jnp.int32, (chunk, chunk), 1)
    tri = (t_idx >= s_idx).astype(BF16)
    row_c = lax.broadcasted_iota(jnp.int32, (chunk, B_DIM), 0)
    sub_t = lax.broadcasted_iota(jnp.int32, (sub, chunk), 0)
    sub_c = lax.broadcasted_iota(jnp.int32, (sub, chunk), 1)
    diag_code = [jnp.where((sub_c >= i * sub) & (sub_c - i * sub <= sub_t), sub_c - i * sub, -1)
                 for i in range(n_sub)]
    halves = []
    half = chunk // 2
    while half >= sub:
        halves.append(half)
        half //= 2

    for ci in range(n_chunks):
        rows = slice(ci * chunk, (ci + 1) * chunk)
        for bi, h in heads:
            cols = slice(h * B_DIM, (h + 1) * B_DIM)
            xq = xq_ref[bi, rows, cols]
            xg = xg_ref[bi, rows, cols]
            v = xi_ref[bi, rows, cols]
            lb = lb_all[:, cols]
            qh = _silu(xq)
            f = lb + (1.0 - lb) * _sigmoid(xf_ref[bi, rows, cols])
            logf = jnp.log(f)
            kh = 1.0 - f

            hi = logf.astype(BF16)
            lo = (logf - hi.astype(F32)).astype(BF16)
            g2 = (_dot(tri, hi) + _dot(tri, lo)) * LOG2_E

            st_slot = bi * B_HEADS + h
            st = st_ref[st_slot]
            o = _dot_nt((qh * jnp.exp2(g2)).astype(BF16), st.astype(BF16))

            slot = ci * len(heads) + st_slot
            g2_scr[slot] = g2
            k_scr[slot] = kh
            a_rows = []
            for i in range(n_sub):
                r0 = i * sub
                gi = g2[r0:r0 + sub]
                qi = qh[r0:r0 + sub]
                a_i = jnp.zeros((sub, chunk), F32)
                for s in range(sub):
                    dec = jnp.exp2(gi - g2_scr[slot, pl.ds(r0 + s, 1), :])
                    col = jnp.sum(qi * k_scr[slot, pl.ds(r0 + s, 1), :] * dec, axis=-1, keepdims=True)
                    a_i = jnp.where(diag_code[i] == s, col, a_i)
                a_rows.append(a_i)
            a = a_rows[0] if n_sub == 1 else jnp.concatenate(a_rows, axis=0)
            for half in halves:
                shift = half.bit_length() - 1
                bounds = [jnp.broadcast_to(g2[r0 + half - 1:r0 + half], (2 * half, B_DIM))
                          for r0 in range(0, chunk, 2 * half)]
                bnd = bounds[0] if len(bounds) == 1 else jnp.concatenate(bounds, axis=0)
                e_half = jnp.exp2(-jnp.abs(g2 - bnd))
                upper = ((row_c >> shift) & 1) == 1
                qo = jnp.where(upper, qh * e_half, 0.0).astype(BF16)
                kk = jnp.where(upper, 0.0, kh * e_half).astype(BF16)
                a_half = _dot_nt(qo, kk)
                if 2 * half < chunk:
                    a_half = jnp.where((t_idx >> (shift + 1)) == (s_idx >> (shift + 1)), a_half, 0.0)
                a = a + a_half
            v16 = v.astype(BF16)
            o = o + _dot(a.astype(BF16), v16)

            g_last = g2[chunk - 1:chunk]
            kd = (kh * jnp.exp2(g_last - g2)).astype(BF16)
            st_ref[st_slot] = jnp.exp2(g_last) * st + _dot_tn(v16, kd)

            o_ref[bi, rows, cols] = _rms(o, g_ref[...]) * _silu(xg)

    @pl.when(t_id == pl.num_programs(1) - 1)
    def _():
        for bi, h in heads:
            sfin_ref[bi, h] = st_ref[bi * B_HEADS + h].T


def _hgrn(z, lb_logits, norm_g, s0, s0_layer, layer, batch, seq, bb, tt, chunk, sub, rider=None):
    zv = z.reshape(batch, seq, N_SPLITS * A_WIDTH)
    n_t = seq // tt
    zspec = lambda col: pl.BlockSpec((bb, tt, B_WIDTH), lambda b, t: (b, t, col))
    sspec = pl.BlockSpec((bb, B_HEADS, B_DIM, B_DIM), lambda b, t: (b, 0, 0, 0))
    s0spec = pl.BlockSpec((None, bb, B_HEADS, B_DIM, B_DIM), lambda b, t: (s0_layer, b, 0, 0, 0))
    depth = lb_logits.shape[0]
    in_specs = [zspec(3), zspec(4), zspec(5), zspec(6),
                pl.BlockSpec((depth, B_WIDTH), lambda b, t: (0, 0)),
                pl.BlockSpec((None, 1, B_DIM), lambda b, t: (layer, 0, 0)),
                s0spec]
    out_specs = [pl.BlockSpec((bb, tt, B_WIDTH), lambda b, t: (b, t, 0)), sspec]
    out_shape = [jax.ShapeDtypeStruct((batch, seq, B_WIDTH), F32),
                 jax.ShapeDtypeStruct((batch, B_HEADS, B_DIM, B_DIM), F32)]
    operands = [zv, zv, zv, zv, lb_logits, norm_g, s0]
    scratch = [pltpu.VMEM((bb * B_HEADS, B_DIM, B_DIM), F32)] \
        + [pltpu.VMEM((tt // chunk * bb * B_HEADS, chunk, B_DIM), F32)] * 2
    if rider is not None:
        z_s, cache_k, cache_v, s_batch, t_s = rider
        assert s_batch == (batch // bb) * n_t
        n_cache = cache_k.shape[2]
        zs_v = z_s.reshape(s_batch, t_s, N_SPLITS * A_WIDTH)
        ckt = jnp.transpose(cache_k, (0, 1, 3, 4, 2))
        cvt = jnp.transpose(cache_v, (0, 1, 3, 4, 2))
        zs_spec = lambda col: pl.BlockSpec((None, t_s, A_WIDTH), lambda b, t: (b * n_t + t, 0, col))
        cspec = pl.BlockSpec((None, None, A_HEADS, A_HEAD_DIM, n_cache), lambda b, t: (layer, b * n_t + t, 0, 0, 0))
        in_specs += [zs_spec(0), zs_spec(1), zs_spec(2), cspec, cspec]
        out_specs.append(pl.BlockSpec((None, t_s, A_WIDTH), lambda b, t: (b * n_t + t, 0, 0)))
        out_shape.append(jax.ShapeDtypeStruct((s_batch, t_s, A_WIDTH), F32))
        operands += [zs_v, zs_v, zs_v, ckt, cvt]
        scratch += [pltpu.VMEM((A_HEADS * t_s, n_cache), F32), pltpu.VMEM((A_HEADS * t_s, t_s), F32)]
    res = pl.pallas_call(
        functools.partial(_hgrn_kernel, layer=layer, chunk=chunk, sub=sub, rider=rider is not None),
        grid=(batch // bb, n_t),
        in_specs=in_specs,
        out_specs=out_specs,
        out_shape=out_shape,
        scratch_shapes=scratch,
        compiler_params=_cparams(2, parallel_first=rider is None),
        name="hgrn_attn_sample" if rider is not None else "hgrn",
    )(*operands)
    o = res[0].reshape(batch * seq, B_WIDTH)
    if rider is not None:
        return o, res[1], res[2].reshape(s_batch * t_s, A_WIDTH)
    return o, res[1]


def kernel(x_prompt, x_sample, cache_attn_k, cache_attn_v, state_hgrn, ff1_pre_g, ff1_w_gate, ff1_w_up, ff1_w_down, ff1_post_g, mix_pre_g, w_in, attn_norm_g, hgrn_lb_logits, hgrn_norm_g, w_out, mix_post_g, ff2_pre_g, ff2_w_gate, ff2_w_up, ff2_w_down, ff2_post_g):
    batch, seq, d = x_prompt.shape
    dec_batch, dec_seq, _ = x_sample.shape
    depth = w_in.shape[0]
    keep = min(DILATIONS[-1] * WINDOW_STEPS, seq)

    row = lambda g: g.reshape(depth, 1, g.shape[-1])
    bf = lambda w: w.astype(BF16)
    ff1 = (row(ff1_pre_g), bf(ff1_w_gate), bf(ff1_w_up), bf(ff1_w_down), row(ff1_post_g))
    ff2 = (row(ff2_pre_g), bf(ff2_w_gate), bf(ff2_w_up), bf(ff2_w_down), row(ff2_post_g))
    mix_pre_g, attn_norm_g, hgrn_norm_g, mix_post_g = map(row, (mix_pre_g, attn_norm_g, hgrn_norm_g, mix_post_g))
    w_in16, w_out16 = bf(w_in), bf(w_out)
    zero_state = jnp.zeros((1, batch, B_HEADS, B_DIM, B_DIM), F32)

    hp = x_prompt.reshape(batch * seq, d)
    hs = x_sample.reshape(dec_batch * dec_seq, d)
    tm_p, tm_s, tf = 512, dec_batch * dec_seq, 256
    zps = []
    outs = {k: [] for k in ("sp", "ks", "vs", "ss")}
    for l in range(depth):
        hp = _ffn(hp, *ff1, l, 2 * tm_p, tf, splits=2)
        zp = _mix_in(hp, mix_pre_g, w_in16, l, 2 * tm_p, splits=2)
        att = _attn_prompt(zp, batch, seq)
        hs, zs = _ffn(hs, *ff1, l, tm_s, tf, proj=(mix_pre_g, w_in16))
        ob, sp, att_s = _hgrn(zp, hgrn_lb_logits, hgrn_norm_g, zero_state, 0, l, batch, seq,
                              1, 8 * HGRN_CHUNK, HGRN_CHUNK, HGRN_SUB,
                              rider=(zs, cache_attn_k, cache_attn_v, dec_batch, dec_seq))
        hp = _ffn(hp, *ff2, l, 2 * tm_p, tf, splits=2, mixer=(att, ob, attn_norm_g, w_out16, mix_post_g))
        zps.append(zp)
        outs["sp"].append(sp)

        ob_s, ss = _hgrn(zs, hgrn_lb_logits, hgrn_norm_g, state_hgrn, l, l, dec_batch, dec_seq,
                         4, dec_seq, dec_seq, dec_seq)
        hs = _ffn(hs, *ff2, l, tm_s, tf, mixer=(att_s, ob_s, attn_norm_g, w_out16, mix_post_g))
        zs3 = zs.reshape(dec_batch, dec_seq, N_SPLITS, A_HEADS, A_HEAD_DIM)
        outs["ks"].append(zs3[:, :, 1])
        outs["vs"].append(zs3[:, :, 2])
        outs["ss"].append(ss)

    kp, vp = _kv_format(zps, batch, seq, keep, 512)
    stack = lambda k: jnp.stack(outs[k])
    return (hp.reshape(batch, seq, d), hs.reshape(dec_batch, dec_seq, d),
            kp, vp, stack("sp"), stack("ks"), stack("vs"), stack("ss"))
```

```python
import functools
import math

import jax
import jax.numpy as jnp
from jax import lax
from jax.experimental import pallas as pl
from jax.experimental.pallas import tpu as pltpu

F32 = jnp.float32
BF16 = jnp.bfloat16

NORM_EPS = 1e-6
LOG2_E = math.log2(math.e)
A_HEADS = 8
A_HEAD_DIM = 64
A_WIDTH = A_HEADS * A_HEAD_DIM
A_PAIR = 2 * A_HEAD_DIM
B_HEADS = 4
B_DIM = 128
B_WIDTH = B_HEADS * B_DIM
DILATIONS = (1, 4, 16)
WINDOW_STEPS = 128
N_SPLITS = 7
HGRN_CHUNK = 128
HGRN_SUB = 8

V7X_VMEM_LIMIT_BYTES = 56 * 1024 * 1024


def _cparams(n_grid_axes, parallel_first=True):
    first = "parallel" if parallel_first else "arbitrary"
    sem = (first,) + ("arbitrary",) * (n_grid_axes - 1)
    return pltpu.CompilerParams(dimension_semantics=sem,
                                vmem_limit_bytes=V7X_VMEM_LIMIT_BYTES)


def _rms(x, g):
    return x * lax.rsqrt(jnp.mean(x * x, axis=-1, keepdims=True) + NORM_EPS) * g


def _sigmoid(x):
    return 1.0 / (1.0 + jnp.exp(-x))


def _silu(x):
    half = 0.5 * x
    return half * jnp.tanh(half) + half


def _dot(a, b):
    return jnp.dot(a, b, preferred_element_type=F32)


def _dot_nt(a, b):
    return lax.dot_general(a, b, (((1,), (1,)), ((), ())), preferred_element_type=F32)


def _dot_tn(a, b):
    return lax.dot_general(a, b, (((0,), (0,)), ((), ())), preferred_element_type=F32)


def _ffn_kernel(*refs, tf, splits, mixer, proj):
    if mixer:
        att_ref, ob_ref, attn_g_ref, wo_ref, mix_post_g_ref = refs[:5]
        refs = refs[5:]
    x_ref, pre_g_ref, wg_ref, wu_ref, wd_ref, post_g_ref = refs[:6]
    refs = refs[6:]
    if proj:
        mix_pre_g_ref, w_in_ref, o_ref, z_ref, xn_ref, acc_ref = refs
    else:
        o_ref, xn_ref, acc_ref = refs
    tm = x_ref.shape[0]
    subs = [slice(i * tm // splits, (i + 1) * tm // splits) for i in range(splits)]
    res_ref = o_ref if mixer else x_ref
    for r in subs:
        if mixer:
            oa = _rms(att_ref[r, :], attn_g_ref[...]).astype(BF16)
            m = _dot(oa, wo_ref[:A_WIDTH, :]) + _dot(ob_ref[r, :].astype(BF16), wo_ref[A_WIDTH:, :])
            o_ref[r, :] = x_ref[r, :] + _rms(m, mix_post_g_ref[...])
        xn_ref[r, :] = _rms(res_ref[r, :], pre_g_ref[...]).astype(BF16)
    ff = wg_ref.shape[-1]
    for c in range(ff // tf):
        cols = slice(c * tf, (c + 1) * tf)
        for r in subs:
            xn = xn_ref[r, :]
            g = _dot(xn, wg_ref[:, cols])
            u = _dot(xn, wu_ref[:, cols])
            a = (_silu(g) * u).astype(BF16)
            part = _dot(a, wd_ref[cols, :])
            if c == 0:
                acc_ref[r, :] = part
            else:
                acc_ref[r, :] += part
    for r in subs:
        o_ref[r, :] = res_ref[r, :] + 0.5 * _rms(acc_ref[r, :], post_g_ref[...])
        if proj:
            xn_ref[r, :] = _rms(o_ref[r, :], mix_pre_g_ref[...]).astype(BF16)
            for c in range(N_SPLITS):
                cols = slice(c * A_WIDTH, (c + 1) * A_WIDTH)
                z_ref[r, cols] = _dot(xn_ref[r, :], w_in_ref[:, cols])


def _resident(block_shape, index_map):
    return pl.BlockSpec(block_shape, index_map, pipeline_mode=pl.Buffered(1))


def _ffn(x, pre_g, wg, wu, wd, post_g, layer, tm, tf, splits=1, mixer=None, proj=None):
    m, d = x.shape
    ff = wg.shape[-1]
    gain = lambda width: _resident((None, 1, width), lambda i: (layer, 0, 0))
    rows = lambda width: pl.BlockSpec((tm, width), lambda i: (i, 0))
    weight = lambda k, n: _resident((None, k, n), lambda i: (layer, 0, 0))
    mixer_specs, proj_specs = [], []
    out_specs, out_shape = rows(d), jax.ShapeDtypeStruct((m, d), F32)
    if mixer is not None:
        mixer_specs = [rows(A_WIDTH), rows(B_WIDTH), gain(A_WIDTH), weight(A_WIDTH + B_WIDTH, d), gain(d)]
    if proj is not None:
        n = proj[1].shape[-1]
        proj_specs = [gain(d), weight(d, n)]
        out_specs, out_shape = [out_specs, rows(n)], [out_shape, jax.ShapeDtypeStruct((m, n), F32)]
    return pl.pallas_call(
        functools.partial(_ffn_kernel, tf=tf, splits=splits, mixer=mixer is not None, proj=proj is not None),
        grid=(m // tm,),
        in_specs=mixer_specs + [rows(d), gain(d), weight(d, ff), weight(d, ff), weight(ff, d), gain(d)] + proj_specs,
        out_specs=out_specs,
        out_shape=out_shape,
        scratch_shapes=[pltpu.VMEM((tm, d), BF16), pltpu.VMEM((tm, d), F32)],
        compiler_params=_cparams(1),
        name=("mix_out_" if mixer is not None else "") + "ffn" + ("_mix_in" if proj is not None else ""),
    )(*(mixer or ()), x, pre_g, wg, wu, wd, post_g, *(proj or ()))


def _mix_in_kernel(h_ref, g_ref, w_ref, z_ref, hn_ref, *, splits):
    tm = h_ref.shape[0]
    subs = [slice(i * tm // splits, (i + 1) * tm // splits) for i in range(splits)]
    for r in subs:
        hn_ref[r, :] = _rms(h_ref[r, :], g_ref[...]).astype(BF16)
    for c in range(N_SPLITS):
        cols = slice(c * A_WIDTH, (c + 1) * A_WIDTH)
        for r in subs:
            z_ref[r, cols] = _dot(hn_ref[r, :], w_ref[:, cols])


def _mix_in(h, g, w_in, layer, tm, splits=1):
    m, d = h.shape
    n = w_in.shape[-1]
    return pl.pallas_call(
        functools.partial(_mix_in_kernel, splits=splits),
        grid=(m // tm,),
        in_specs=[
            pl.BlockSpec((tm, d), lambda i: (i, 0)),
            _resident((None, 1, d), lambda i: (layer, 0, 0)),
            _resident((None, d, n), lambda i: (layer, 0, 0)),
        ],
        out_specs=pl.BlockSpec((tm, n), lambda i: (i, 0)),
        out_shape=jax.ShapeDtypeStruct((m, n), F32),
        scratch_shapes=[pltpu.VMEM((tm, d), BF16)],
        compiler_params=_cparams(1),
        name="mix_in",
    )(h, g, w_in)


def _attn_block(q, k_prev, k_own, v_prev, v_own, bias, first_head):
    w = q.shape[0]
    q = q * (LOG2_E / math.sqrt(A_HEAD_DIM))
    q2 = jnp.concatenate([jnp.where(first_head, q, 0.0), jnp.where(first_head, 0.0, q)], axis=0)
    kcat = jnp.concatenate([k_prev, k_own], axis=0).astype(BF16)
    vcat = jnp.concatenate([v_prev, v_own], axis=0).astype(BF16)
    s = _dot_nt(q2.astype(BF16), kcat) + jnp.concatenate([bias, bias], axis=0)
    m_row = jnp.max(s, axis=-1, keepdims=True)
    p = jnp.exp2(s - m_row)
    l_row = jnp.sum(p, axis=-1, keepdims=True)
    pv = _dot(p.astype(BF16), vcat)
    pick = lambda x: jnp.where(first_head, x[:w], x[w:])
    return (pick(jnp.broadcast_to(m_row, (2 * w, A_PAIR))),
            pick(jnp.broadcast_to(l_row, (2 * w, A_PAIR))), pick(pv))


def _attn_merge(m_o, l_o, acc_o, m_b, l_b, acc_b):
    m_n = jnp.maximum(m_o, m_b)
    a_o = jnp.exp2(m_o - m_n)
    a_b = jnp.exp2(m_b - m_n)
    return m_n, a_o * l_o + a_b * l_b, a_o * acc_o + a_b * acc_b


def _attn_prompt_kernel(q_ref, k_ref, v_ref, o_ref, m_ref, l_ref, acc_ref, q4_ref, k4_ref, v4_ref,
                        m4_ref, l4_ref, acc4_ref, bias_ref, *, unroll):
    w = WINDOW_STEPS
    seq = q_ref.shape[0]
    d1, d2 = DILATIONS[1], DILATIONS[2]
    ratio = d2 // d1
    per_res = seq // d1
    nb1 = per_res // w
    nb2 = seq // (d2 * w)
    first_head = lax.broadcasted_iota(jnp.int32, (w, A_PAIR), 1) < A_HEAD_DIM
    qi = lax.broadcasted_iota(jnp.int32, (w, 2 * w), 0)
    ki = lax.broadcasted_iota(jnp.int32, (w, 2 * w), 1)
    dist = w + qi - ki
    band = (dist >= 0) & (dist <= w)
    bias_ref[0] = jnp.where(band & (ki >= w), 0.0, -jnp.inf)
    bias_ref[1] = jnp.where(band, 0.0, -jnp.inf)

    def pattern0(c, carry):
        rows = pl.ds(pl.multiple_of(c * w, w), w)
        prev = pl.ds(pl.multiple_of(jnp.maximum(c - 1, 0) * w, w), w)
        m_b, l_b, acc_b = _attn_block(q_ref[rows, :], k_ref[prev, :], k_ref[rows, :], v_ref[prev, :],
                                      v_ref[rows, :], bias_ref[jnp.minimum(c, 1)], first_head)
        m_ref[rows, :] = m_b
        l_ref[rows, :] = l_b
        acc_ref[rows, :] = acc_b
        return carry

    lax.fori_loop(0, seq // w, pattern0, 0, unroll=4 * unroll)

    for r in range(d1):
        dst = pl.ds(r * per_res, per_res)
        src = pl.ds(r, per_res, stride=d1)
        for nat, grouped in ((q_ref, q4_ref), (k_ref, k4_ref), (v_ref, v4_ref),
                             (m_ref, m4_ref), (l_ref, l4_ref), (acc_ref, acc4_ref)):
            grouped[dst, :] = nat[src, :]

    def pattern1(idx, carry):
        has_prev = jnp.minimum(idx & (nb1 - 1), 1)
        rows = pl.ds(pl.multiple_of(idx * w, w), w)
        prev = pl.ds(pl.multiple_of((idx - has_prev) * w, w), w)
        blk = _attn_block(q4_ref[rows, :], k4_ref[prev, :], k4_ref[rows, :], v4_ref[prev, :],
                          v4_ref[rows, :], bias_ref[has_prev], first_head)
        m_n, l_n, acc_n = _attn_merge(m4_ref[rows, :], l4_ref[rows, :], acc4_ref[rows, :], *blk)
        m4_ref[rows, :] = m_n
        l4_ref[rows, :] = l_n
        acc4_ref[rows, :] = acc_n
        return carry

    assert nb1 & (nb1 - 1) == 0 and nb2 & (nb2 - 1) == 0 and ratio & (ratio - 1) == 0
    lax.fori_loop(0, seq // w, pattern1, 0, unroll=4 * unroll)

    def pattern2(idx, carry):
        a = idx & (ratio - 1)
        c = (idx >> (ratio.bit_length() - 1)) & (nb2 - 1)
        r = idx >> ((ratio * nb2).bit_length() - 1)
        has_prev = jnp.minimum(c, 1)
        start = r * per_res + c * (w * ratio) + a
        rows = pl.ds(start, w, stride=ratio)
        prev = pl.ds(start - has_prev * (w * ratio), w, stride=ratio)
        blk = _attn_block(q4_ref[rows, :], k4_ref[prev, :], k4_ref[rows, :], v4_ref[prev, :],
                          v4_ref[rows, :], bias_ref[has_prev], first_head)
        _, l_n, acc_n = _attn_merge(m4_ref[rows, :], l4_ref[rows, :], acc4_ref[rows, :], *blk)
        acc_ref[rows, :] = acc_n / l_n
        return carry

    lax.fori_loop(0, seq // w, pattern2, 0, unroll=unroll)

    for r in range(d1):
        o_ref[pl.ds(r, per_res, stride=d1), :] = acc_ref[pl.ds(r * per_res, per_res), :]


def _attn_prompt(z, batch, seq, unroll=8):
    assert DILATIONS[0] == 1 and len(DILATIONS) == 3 and DILATIONS[2] % DILATIONS[1] == 0
    assert seq % (DILATIONS[-1] * WINDOW_STEPS) == 0
    zv = z.reshape(batch, seq, N_SPLITS * A_WIDTH)
    n_pairs = A_WIDTH // A_PAIR
    spec = lambda split: pl.BlockSpec((None, seq, A_PAIR), lambda b, p: (b, 0, split * n_pairs + p))
    o = pl.pallas_call(
        functools.partial(_attn_prompt_kernel, unroll=unroll),
        grid=(batch, n_pairs),
        in_specs=[spec(0), spec(1), spec(2)],
        out_specs=pl.BlockSpec((None, seq, A_PAIR), lambda b, p: (b, 0, p)),
        out_shape=jax.ShapeDtypeStruct((batch, seq, A_WIDTH), F32),
        scratch_shapes=[pltpu.VMEM((seq, A_PAIR), F32)] * 9
        + [pltpu.VMEM((2, WINDOW_STEPS, 2 * WINDOW_STEPS), F32)],
        compiler_params=_cparams(2),
        name="attn_prompt",
    )(zv, zv, zv)
    return o.reshape(batch * seq, A_WIDTH)


def _pattern_count(delta):
    cnt = jnp.zeros(delta.shape, F32)
    for dil in DILATIONS:
        hit = (delta >= 0) & ((delta & (dil - 1)) == 0) & (delta <= dil * WINDOW_STEPS)
        cnt = cnt + hit.astype(F32)
    return cnt


def _attn_sample_weights(t, n_cache, w_cache_ref, w_new_ref):
    hq = A_HEADS * t
    assert t & (t - 1) == 0

    def weights(n_cols, first_pos):
        query = lax.broadcasted_iota(jnp.int32, (hq, n_cols), 0) & (t - 1)
        key = first_pos + lax.broadcasted_iota(jnp.int32, (hq, n_cols), 1)
        return _pattern_count(n_cache + query - key)

    w_cache_ref[...] = weights(n_cache, 0)
    w_new_ref[...] = weights(t, n_cache)


def _attn_sample_kernel(q_ref, kn_ref, vn_ref, kt_ref, vt_ref, o_ref, w_cache_ref, w_new_ref):
    t = q_ref.shape[0]
    hq = A_HEADS * t
    n_cache = kt_ref.shape[-1]
    t_shift = t.bit_length() - 1
    w_cache = w_cache_ref[...]
    w_new = w_new_ref[...]

    row_head = lax.broadcasted_iota(jnp.int32, (hq, A_WIDTH), 0) >> t_shift
    lane_head = lax.broadcasted_iota(jnp.int32, (hq, A_WIDTH), 1) >> (A_HEAD_DIM.bit_length() - 1)
    own_head = row_head == lane_head
    q = q_ref[...] * (1.0 / math.sqrt(A_HEAD_DIM))
    q_bd = jnp.where(own_head, jnp.concatenate([q] * A_HEADS, axis=0), 0.0).astype(BF16)

    kt = kt_ref[...].reshape(A_WIDTH, n_cache).astype(BF16)
    vt = vt_ref[...].reshape(A_WIDTH, n_cache).astype(BF16)
    s_c = jnp.where(w_cache > 0, _dot(q_bd, kt), -jnp.inf)
    s_n = jnp.where(w_new > 0, _dot_nt(q_bd, kn_ref[...].astype(BF16)), -jnp.inf)
    m = jnp.maximum(jnp.max(s_c, axis=-1, keepdims=True), jnp.max(s_n, axis=-1, keepdims=True))
    p_c = w_cache * jnp.exp(s_c - m)
    p_n = w_new * jnp.exp(s_n - m)
    l = jnp.sum(p_c, axis=-1, keepdims=True) + jnp.sum(p_n, axis=-1, keepdims=True)
    r = (_dot_nt(p_c.astype(BF16), vt) + _dot(p_n.astype(BF16), vn_ref[...].astype(BF16))) / l
    r = jnp.where(own_head, r, 0.0)
    out = r[0:t]
    for h in range(1, A_HEADS):
        out = out + r[h * t:(h + 1) * t]
    o_ref[...] = out


def _kv_format_kernel(*refs, depth):
    k_refs, v_refs = refs[:depth], refs[depth:2 * depth]
    ko_ref, vo_ref = refs[2 * depth:]
    tm = ko_ref.shape[-1]
    for lyr in range(depth):
        @pl.when(pl.program_id(0) == lyr)
        def _(lyr=lyr):
            for src, dst in ((k_refs[lyr], ko_ref), (v_refs[lyr], vo_ref)):
                dst[...] = src[...].T.reshape(A_HEADS, A_HEAD_DIM, tm)


def _kv_format(zs, batch, seq, keep, tm):
    depth = len(zs)
    zvs = [z.reshape(batch, seq, N_SPLITS * A_WIDTH) for z in zs]
    t0, nt = (seq - keep) // tm, keep // tm

    def in_spec(lyr, split):
        def index(l, b, t):
            before, after = l < lyr, l > lyr
            bb = jnp.where(before, 0, jnp.where(after, batch - 1, b))
            tt = jnp.where(before, 0, jnp.where(after, nt - 1, t))
            return (bb, t0 + tt, split)
        return pl.BlockSpec((None, tm, A_WIDTH), index)

    out_spec = pl.BlockSpec((None, None, A_HEADS, A_HEAD_DIM, tm), lambda l, b, t: (l, b, 0, 0, t))
    shape = jax.ShapeDtypeStruct((depth, batch, A_HEADS, A_HEAD_DIM, keep), F32)
    kt, vt = pl.pallas_call(
        functools.partial(_kv_format_kernel, depth=depth),
        grid=(depth, batch, nt),
        in_specs=[in_spec(l, 1) for l in range(depth)] + [in_spec(l, 2) for l in range(depth)],
        out_specs=[out_spec, out_spec],
        out_shape=[shape, shape],
        compiler_params=_cparams(3, parallel_first=False),
        name="kv_format",
    )(*zvs, *zvs)
    return jnp.transpose(kt, (0, 1, 4, 2, 3)), jnp.transpose(vt, (0, 1, 4, 2, 3))


def _hgrn_kernel(*refs, layer, chunk, sub, rider):
    if rider:
        (xq_ref, xf_ref, xi_ref, xg_ref, lbl_ref, g_ref, s0_ref, sq_ref, skn_ref, svn_ref, skt_ref, svt_ref,
         o_ref, sfin_ref, so_ref, st_ref, g2_scr, k_scr, wc_scr, wn_scr) = refs

        @pl.when((pl.program_id(0) == 0) & (pl.program_id(1) == 0))
        def _():
            _attn_sample_weights(sq_ref.shape[0], skt_ref.shape[-1], wc_scr, wn_scr)

        @pl.when(pl.program_id(1) >= 0)
        def _():
            _attn_sample_kernel(sq_ref, skn_ref, svn_ref, skt_ref, svt_ref, so_ref, wc_scr, wn_scr)
    else:
        xq_ref, xf_ref, xi_ref, xg_ref, lbl_ref, g_ref, s0_ref, o_ref, sfin_ref, st_ref, g2_scr, k_scr = refs
    t_id = pl.program_id(1)
    n_seqs = xq_ref.shape[0]
    heads = [(bi, h) for bi in range(n_seqs) for h in range(B_HEADS)]

    @pl.when(t_id == 0)
    def _():
        for bi, h in heads:
            st_ref[bi * B_HEADS + h] = s0_ref[bi, h].T

    logits = lbl_ref[...]
    e = jnp.exp(logits - jnp.max(logits, axis=0, keepdims=True))
    soft = e / jnp.sum(e, axis=0, keepdims=True)
    lb_all = soft[0:1] - soft[0:1]
    for l in range(1, layer + 1):
        lb_all = lb_all + soft[l:l + 1]

    n_chunks = xq_ref.shape[1] // chunk
    n_sub = chunk // sub
    t_idx = lax.broadcasted_iota(jnp.int32, (chunk, chunk), 0)
    s_idx = lax.broadcasted_iota(jnp.int32, (chunk, chunk), 1)
    tri = (t_idx >= s_idx).astype(BF16)
    row_c = lax.broadcasted_iota(jnp.int32, (chunk, B_DIM), 0)
    sub_t = lax.broadcasted_iota(jnp.int32, (sub, chunk), 0)
    sub_c = lax.broadcasted_iota(jnp.int32, (sub, chunk), 1)
    diag_code = [jnp.where((sub_c >= i * sub) & (sub_c - i * sub <= sub_t), sub_c - i * sub, -1)
                 for i in range(n_sub)]
    halves = []
    half = chunk // 2
    while half >= sub:
        halves.append(half)
        half //= 2

    for ci in range(n_chunks):
        rows = slice(ci * chunk, (ci + 1) * chunk)
        for bi, h in heads:
            cols = slice(h * B_DIM, (h + 1) * B_DIM)
            xq = xq_ref[bi, rows, cols]
            xg = xg_ref[bi, rows, cols]
            v = xi_ref[bi, rows, cols]
            lb = lb_all[:, cols]
            qh = _silu(xq)
            f = lb + (1.0 - lb) * _sigmoid(xf_ref[bi, rows, cols])
            logf = jnp.log(f)
            kh = 1.0 - f

            hi = logf.astype(BF16)
            lo = (logf - hi.astype(F32)).astype(BF16)
            g2 = (_dot(tri, hi) + _dot(tri, lo)) * LOG2_E

            st_slot = bi * B_HEADS + h
            st = st_ref[st_slot]
            o = _dot_nt((qh * jnp.exp2(g2)).astype(BF16), st.astype(BF16))

            slot = ci * len(heads) + st_slot
            g2_scr[slot] = g2
            k_scr[slot] = kh
            a_rows = []
            for i in range(n_sub):
                r0 = i * sub
                gi = g2[r0:r0 + sub]
                qi = qh[r0:r0 + sub]
                a_i = jnp.zeros((sub, chunk), F32)
                for s in range(sub):
                    dec = jnp.exp2(gi - g2_scr[slot, pl.ds(r0 + s, 1), :])
                    col = jnp.sum(qi * k_scr[slot, pl.ds(r0 + s, 1), :] * dec, axis=-1, keepdims=True)
                    a_i = jnp.where(diag_code[i] == s, col, a_i)
                a_rows.append(a_i)
            a = a_rows[0] if n_sub == 1 else jnp.concatenate(a_rows, axis=0)
            for half in halves:
                shift = half.bit_length() - 1
                bounds = [jnp.broadcast_to(g2[r0 + half - 1:r0 + half], (2 * half, B_DIM))
                          for r0 in range(0, chunk, 2 * half)]
                bnd = bounds[0] if len(bounds) == 1 else jnp.concatenate(bounds, axis=0)
                e_half = jnp.exp2(-jnp.abs(g2 - bnd))
                upper = ((row_c >> shift) & 1) == 1
                qo = jnp.where(upper, qh * e_half, 0.0).astype(BF16)
                kk = jnp.where(upper, 0.0, kh * e_half).astype(BF16)
                a_half = _dot_nt(qo, kk)
                if 2 * half < chunk:
                    a_half = jnp.where((t_idx >> (shift + 1)) == (s_idx >> (shift + 1)), a_half, 0.0)
                a = a + a_half
            v16 = v.astype(BF16)
            o = o + _dot(a.astype(BF16), v16)

            g_last = g2[chunk - 1:chunk]
            kd = (kh * jnp.exp2(g_last - g2)).astype(BF16)
            st_ref[st_slot] = jnp.exp2(g_last) * st + _dot_tn(v16, kd)

            o_ref[bi, rows, cols] = _rms(o, g_ref[...]) * _silu(xg)

    @pl.when(t_id == pl.num_programs(1) - 1)
    def _():
        for bi, h in heads:
            sfin_ref[bi, h] = st_ref[bi * B_HEADS + h].T


def _hgrn(z, lb_logits, norm_g, s0, s0_layer, layer, batch, seq, bb, tt, chunk, sub, rider=None):
    zv = z.reshape(batch, seq, N_SPLITS * A_WIDTH)
    n_t = seq // tt
    zspec = lambda col: pl.BlockSpec((bb, tt, B_WIDTH), lambda b, t: (b, t, col))
    sspec = pl.BlockSpec((bb, B_HEADS, B_DIM, B_DIM), lambda b, t: (b, 0, 0, 0))
    s0spec = pl.BlockSpec((None, bb, B_HEADS, B_DIM, B_DIM), lambda b, t: (s0_layer, b, 0, 0, 0))
    depth = lb_logits.shape[0]
    in_specs = [zspec(3), zspec(4), zspec(5), zspec(6),
                pl.BlockSpec((depth, B_WIDTH), lambda b, t: (0, 0)),
                pl.BlockSpec((None, 1, B_DIM), lambda b, t: (layer, 0, 0)),
                s0spec]
    out_specs = [pl.BlockSpec((bb, tt, B_WIDTH), lambda b, t: (b, t, 0)), sspec]
    out_shape = [jax.ShapeDtypeStruct((batch, seq, B_WIDTH), F32),
                 jax.ShapeDtypeStruct((batch, B_HEADS, B_DIM, B_DIM), F32)]
    operands = [zv, zv, zv, zv, lb_logits, norm_g, s0]
    scratch = [pltpu.VMEM((bb * B_HEADS, B_DIM, B_DIM), F32)] \
        + [pltpu.VMEM((tt // chunk * bb * B_HEADS, chunk, B_DIM), F32)] * 2
    if rider is not None:
        z_s, cache_k, cache_v, s_batch, t_s = rider
        assert s_batch == (batch // bb) * n_t
        n_cache = cache_k.shape[2]
        zs_v = z_s.reshape(s_batch, t_s, N_SPLITS * A_WIDTH)
        ckt = jnp.transpose(cache_k, (0, 1, 3, 4, 2))
        cvt = jnp.transpose(cache_v, (0, 1, 3, 4, 2))
        zs_spec = lambda col: pl.BlockSpec((None, t_s, A_WIDTH), lambda b, t: (b * n_t + t, 0, col))
        cspec = pl.BlockSpec((None, None, A_HEADS, A_HEAD_DIM, n_cache), lambda b, t: (layer, b * n_t + t, 0, 0, 0))
        in_specs += [zs_spec(0), zs_spec(1), zs_spec(2), cspec, cspec]
        out_specs.append(pl.BlockSpec((None, t_s, A_WIDTH), lambda b, t: (b * n_t + t, 0, 0)))
        out_shape.append(jax.ShapeDtypeStruct((s_batch, t_s, A_WIDTH), F32))
        operands += [zs_v, zs_v, zs_v, ckt, cvt]
        scratch += [pltpu.VMEM((A_HEADS * t_s, n_cache), F32), pltpu.VMEM((A_HEADS * t_s, t_s), F32)]
    res = pl.pallas_call(
        functools.partial(_hgrn_kernel, layer=layer, chunk=chunk, sub=sub, rider=rider is not None),
        grid=(batch // bb, n_t),
        in_specs=in_specs,
        out_specs=out_specs,
        out_shape=out_shape,
        scratch_shapes=scratch,
        compiler_params=_cparams(2, parallel_first=rider is None),
        name="hgrn_attn_sample" if rider is not None else "hgrn",
    )(*operands)
    o = res[0].reshape(batch * seq, B_WIDTH)
    if rider is not None:
        return o, res[1], res[2].reshape(s_batch * t_s, A_WIDTH)
    return o, res[1]


def kernel(x_prompt, x_sample, cache_attn_k, cache_attn_v, state_hgrn, ff1_pre_g, ff1_w_gate, ff1_w_up, ff1_w_down, ff1_post_g, mix_pre_g, w_in, attn_norm_g, hgrn_lb_logits, hgrn_norm_g, w_out, mix_post_g, ff2_pre_g, ff2_w_gate, ff2_w_up, ff2_w_down, ff2_post_g):
    batch, seq, d = x_prompt.shape
    dec_batch, dec_seq, _ = x_sample.shape
    depth = w_in.shape[0]
    keep = min(DILATIONS[-1] * WINDOW_STEPS, seq)

    row = lambda g: g.reshape(depth, 1, g.shape[-1])
    bf = lambda w: w.astype(BF16)
    ff1 = (row(ff1_pre_g), bf(ff1_w_gate), bf(ff1_w_up), bf(ff1_w_down), row(ff1_post_g))
    ff2 = (row(ff2_pre_g), bf(ff2_w_gate), bf(ff2_w_up), bf(ff2_w_down), row(ff2_post_g))
    mix_pre_g, attn_norm_g, hgrn_norm_g, mix_post_g = map(row, (mix_pre_g, attn_norm_g, hgrn_norm_g, mix_post_g))
    w_in16, w_out16 = bf(w_in), bf(w_out)
    zero_state = jnp.zeros((1, batch, B_HEADS, B_DIM, B_DIM), F32)

    hp = x_prompt.reshape(batch * seq, d)
    hs = x_sample.reshape(dec_batch * dec_seq, d)
    tm_p, tm_s, tf = 512, dec_batch * dec_seq, 256
    zps = []
    outs = {k: [] for k in ("sp", "ks", "vs", "ss")}
    for l in range(depth):
        hp = _ffn(hp, *ff1, l, 2 * tm_p, tf, splits=2)
        zp = _mix_in(hp, mix_pre_g, w_in16, l, 2 * tm_p, splits=2)
        att = _attn_prompt(zp, batch, seq)
        hs, zs = _ffn(hs, *ff1, l, tm_s, tf, proj=(mix_pre_g, w_in16))
        ob, sp, att_s = _hgrn(zp, hgrn_lb_logits, hgrn_norm_g, zero_state, 0, l, batch, seq,
                              1, 4 * HGRN_CHUNK, HGRN_CHUNK, HGRN_SUB,
                              rider=(zs, cache_attn_k, cache_attn_v, dec_batch, dec_seq))
        hp = _ffn(hp, *ff2, l, 2 * tm_p, tf, splits=2, mixer=(att, ob, attn_norm_g, w_out16, mix_post_g))
        zps.append(zp)
        outs["sp"].append(sp)

        ob_s, ss = _hgrn(zs, hgrn_lb_logits, hgrn_norm_g, state_hgrn, l, l, dec_batch, dec_seq,
                         4, dec_seq, dec_seq, dec_seq)
        hs = _ffn(hs, *ff2, l, tm_s, tf, mixer=(att_s, ob_s, attn_norm_g, w_out16, mix_post_g))
        zs3 = zs.reshape(dec_batch, dec_seq, N_SPLITS, A_HEADS, A_HEAD_DIM)
        outs["ks"].append(zs3[:, :, 1])
        outs["vs"].append(zs3[:, :, 2])
        outs["ss"].append(ss)

    kp, vp = _kv_format(zps, batch, seq, keep, 512)
    stack = lambda k: jnp.stack(outs[k])
    return (hp.reshape(batch, seq, d), hs.reshape(dec_batch, dec_seq, d),
            kp, vp, stack("sp"), stack("ks"), stack("vs"), stack("ss"))
```

```python
import functools
import math

import jax
import jax.numpy as jnp
from jax import lax
from jax.experimental import pallas as pl
from jax.experimental.pallas import tpu as pltpu

F32 = jnp.float32
BF16 = jnp.bfloat16

NORM_EPS = 1e-6
LOG2_E = math.log2(math.e)
A_HEADS = 8
A_HEAD_DIM = 64
A_WIDTH = A_HEADS * A_HEAD_DIM
A_PAIR = 2 * A_HEAD_DIM
B_HEADS = 4
B_DIM = 128
B_WIDTH = B_HEADS * B_DIM
DILATIONS = (1, 4, 16)
WINDOW_STEPS = 128
N_SPLITS = 7
HGRN_CHUNK = 128
HGRN_SUB = 8

V7X_VMEM_LIMIT_BYTES = 56 * 1024 * 1024

DENSE_ROW_TILE = 1024
DENSE_SUB_TILES = 2
FFN_COL_TILE = 256
FFN_COL_TILE_SAMPLE = 1408
HGRN_CHUNKS_PER_STEP = 4
HGRN_SAMPLE_SEQS_PER_STEP = 8
KV_ROW_TILE = 512


def _cparams(n_grid_axes, parallel_first=True):
    first = "parallel" if parallel_first else "arbitrary"
    sem = (first,) + ("arbitrary",) * (n_grid_axes - 1)
    return pltpu.CompilerParams(dimension_semantics=sem,
                                vmem_limit_bytes=V7X_VMEM_LIMIT_BYTES)


def _rms(x, g):
    return x * lax.rsqrt(jnp.mean(x * x, axis=-1, keepdims=True) + NORM_EPS) * g


def _sigmoid(x):
    return 1.0 / (1.0 + jnp.exp(-x))


def _silu(x):
    half = 0.5 * x
    return half * jnp.tanh(half) + half


def _dot(a, b):
    return jnp.dot(a, b, preferred_element_type=F32)


def _dot_nt(a, b):
    return lax.dot_general(a, b, (((1,), (1,)), ((), ())), preferred_element_type=F32)


def _dot_tn(a, b):
    return lax.dot_general(a, b, (((0,), (0,)), ((), ())), preferred_element_type=F32)


def _ffn_kernel(*refs, tf, splits, mixer, proj):
    if mixer:
        att_ref, ob_ref, attn_g_ref, wo_ref, mix_post_g_ref = refs[:5]
        refs = refs[5:]
    x_ref, pre_g_ref, wg_ref, wu_ref, wd_ref, post_g_ref = refs[:6]
    refs = refs[6:]
    if proj:
        mix_pre_g_ref, w_in_ref, o_ref, z_ref, xn_ref, acc_ref = refs
    else:
        o_ref, xn_ref, acc_ref = refs
    tm = x_ref.shape[0]
    subs = [slice(i * tm // splits, (i + 1) * tm // splits) for i in range(splits)]
    res_ref = o_ref if mixer else x_ref
    for r in subs:
        if mixer:
            oa = _rms(att_ref[r, :], attn_g_ref[...]).astype(BF16)
            m = _dot(oa, wo_ref[:A_WIDTH, :]) + _dot(ob_ref[r, :].astype(BF16), wo_ref[A_WIDTH:, :])
            o_ref[r, :] = x_ref[r, :] + _rms(m, mix_post_g_ref[...])
        xn_ref[r, :] = _rms(res_ref[r, :], pre_g_ref[...]).astype(BF16)
    ff = wg_ref.shape[-1]
    for c in range(ff // tf):
        cols = slice(c * tf, (c + 1) * tf)
        for r in subs:
            xn = xn_ref[r, :]
            g = _dot(xn, wg_ref[:, cols])
            u = _dot(xn, wu_ref[:, cols])
            a = (_silu(g) * u).astype(BF16)
            part = _dot(a, wd_ref[cols, :])
            if c == 0:
                acc_ref[r, :] = part
            else:
                acc_ref[r, :] += part
    for r in subs:
        o_ref[r, :] = res_ref[r, :] + 0.5 * _rms(acc_ref[r, :], post_g_ref[...])
        if proj:
            xn_ref[r, :] = _rms(o_ref[r, :], mix_pre_g_ref[...]).astype(BF16)
            for c in range(N_SPLITS):
                cols = slice(c * A_WIDTH, (c + 1) * A_WIDTH)
                z_ref[r, cols] = _dot(xn_ref[r, :], w_in_ref[:, cols])


def _resident(block_shape, index_map):
    return pl.BlockSpec(block_shape, index_map, pipeline_mode=pl.Buffered(1))


def _ffn(x, pre_g, wg, wu, wd, post_g, layer, tm, tf, splits=1, mixer=None, proj=None):
    m, d = x.shape
    ff = wg.shape[-1]
    gain = lambda width: _resident((None, 1, width), lambda i: (layer, 0, 0))
    rows = lambda width: pl.BlockSpec((tm, width), lambda i: (i, 0))
    weight = lambda k, n: _resident((None, k, n), lambda i: (layer, 0, 0))
    mixer_specs, proj_specs = [], []
    out_specs, out_shape = rows(d), jax.ShapeDtypeStruct((m, d), F32)
    if mixer is not None:
        mixer_specs = [rows(A_WIDTH), rows(B_WIDTH), gain(A_WIDTH), weight(A_WIDTH + B_WIDTH, d), gain(d)]
    if proj is not None:
        n = proj[1].shape[-1]
        proj_specs = [gain(d), weight(d, n)]
        out_specs, out_shape = [out_specs, rows(n)], [out_shape, jax.ShapeDtypeStruct((m, n), F32)]
    return pl.pallas_call(
        functools.partial(_ffn_kernel, tf=tf, splits=splits, mixer=mixer is not None, proj=proj is not None),
        grid=(m // tm,),
        in_specs=mixer_specs + [rows(d), gain(d), weight(d, ff), weight(d, ff), weight(ff, d), gain(d)] + proj_specs,
        out_specs=out_specs,
        out_shape=out_shape,
        scratch_shapes=[pltpu.VMEM((tm, d), BF16), pltpu.VMEM((tm, d), F32)],
        compiler_params=_cparams(1),
        name=("mix_out_" if mixer is not None else "") + "ffn" + ("_mix_in" if proj is not None else ""),
    )(*(mixer or ()), x, pre_g, wg, wu, wd, post_g, *(proj or ()))


def _mix_in_kernel(h_ref, g_ref, w_ref, z_ref, hn_ref, *, splits):
    tm = h_ref.shape[0]
    subs = [slice(i * tm // splits, (i + 1) * tm // splits) for i in range(splits)]
    for r in subs:
        hn_ref[r, :] = _rms(h_ref[r, :], g_ref[...]).astype(BF16)
    for c in range(N_SPLITS):
        cols = slice(c * A_WIDTH, (c + 1) * A_WIDTH)
        for r in subs:
            z_ref[r, cols] = _dot(hn_ref[r, :], w_ref[:, cols])


def _mix_in(h, g, w_in, layer, tm, splits=1):
    m, d = h.shape
    n = w_in.shape[-1]
    return pl.pallas_call(
        functools.partial(_mix_in_kernel, splits=splits),
        grid=(m // tm,),
        in_specs=[
            pl.BlockSpec((tm, d), lambda i: (i, 0)),
            _resident((None, 1, d), lambda i: (layer, 0, 0)),
            _resident((None, d, n), lambda i: (layer, 0, 0)),
        ],
        out_specs=pl.BlockSpec((tm, n), lambda i: (i, 0)),
        out_shape=jax.ShapeDtypeStruct((m, n), F32),
        scratch_shapes=[pltpu.VMEM((tm, d), BF16)],
        compiler_params=_cparams(1),
        name="mix_in",
    )(h, g, w_in)


def _attn_block(q, k_prev, k_own, v_prev, v_own, bias, first_head):
    w = q.shape[0]
    q = q * (LOG2_E / math.sqrt(A_HEAD_DIM))
    q2 = jnp.concatenate([jnp.where(first_head, q, 0.0), jnp.where(first_head, 0.0, q)], axis=0)
    kcat = jnp.concatenate([k_prev, k_own], axis=0).astype(BF16)
    vcat = jnp.concatenate([v_prev, v_own], axis=0).astype(BF16)
    s = _dot_nt(q2.astype(BF16), kcat) + jnp.concatenate([bias, bias], axis=0)
    m_row = jnp.max(s, axis=-1, keepdims=True)
    p = jnp.exp2(s - m_row)
    l_row = jnp.sum(p, axis=-1, keepdims=True)
    pv = _dot(p.astype(BF16), vcat)
    pick = lambda x: jnp.where(first_head, x[:w], x[w:])
    return (pick(jnp.broadcast_to(m_row, (2 * w, A_PAIR))),
            pick(jnp.broadcast_to(l_row, (2 * w, A_PAIR))), pick(pv))


def _attn_merge(m_o, l_o, acc_o, m_b, l_b, acc_b):
    m_n = jnp.maximum(m_o, m_b)
    a_o = jnp.exp2(m_o - m_n)
    a_b = jnp.exp2(m_b - m_n)
    return m_n, a_o * l_o + a_b * l_b, a_o * acc_o + a_b * acc_b


def _attn_prompt_kernel(q_ref, k_ref, v_ref, o_ref, m_ref, l_ref, acc_ref, q4_ref, k4_ref, v4_ref,
                        m4_ref, l4_ref, acc4_ref, bias_ref, *, unroll):
    w = WINDOW_STEPS
    seq = q_ref.shape[0]
    d1, d2 = DILATIONS[1], DILATIONS[2]
    ratio = d2 // d1
    per_res = seq // d1
    nb1 = per_res // w
    nb2 = seq // (d2 * w)
    first_head = lax.broadcasted_iota(jnp.int32, (w, A_PAIR), 1) < A_HEAD_DIM
    qi = lax.broadcasted_iota(jnp.int32, (w, 2 * w), 0)
    ki = lax.broadcasted_iota(jnp.int32, (w, 2 * w), 1)
    dist = w + qi - ki
    band = (dist >= 0) & (dist <= w)
    bias_ref[0] = jnp.where(band & (ki >= w), 0.0, -jnp.inf)
    bias_ref[1] = jnp.where(band, 0.0, -jnp.inf)

    def pattern0(c, carry):
        rows = pl.ds(pl.multiple_of(c * w, w), w)
        prev = pl.ds(pl.multiple_of(jnp.maximum(c - 1, 0) * w, w), w)
        m_b, l_b, acc_b = _attn_block(q_ref[rows, :], k_ref[prev, :], k_ref[rows, :], v_ref[prev, :],
                                      v_ref[rows, :], bias_ref[jnp.minimum(c, 1)], first_head)
        m_ref[rows, :] = m_b
        l_ref[rows, :] = l_b
        acc_ref[rows, :] = acc_b
        return carry

    lax.fori_loop(0, seq // w, pattern0, 0, unroll=4 * unroll)

    for r in range(d1):
        dst = pl.ds(r * per_res, per_res)
        src = pl.ds(r, per_res, stride=d1)
        for nat, grouped in ((q_ref, q4_ref), (k_ref, k4_ref), (v_ref, v4_ref),
                             (m_ref, m4_ref), (l_ref, l4_ref), (acc_ref, acc4_ref)):
            grouped[dst, :] = nat[src, :]

    def pattern1(idx, carry):
        has_prev = jnp.minimum(idx & (nb1 - 1), 1)
        rows = pl.ds(pl.multiple_of(idx * w, w), w)
        prev = pl.ds(pl.multiple_of((idx - has_prev) * w, w), w)
        blk = _attn_block(q4_ref[rows, :], k4_ref[prev, :], k4_ref[rows, :], v4_ref[prev, :],
                          v4_ref[rows, :], bias_ref[has_prev], first_head)
        m_n, l_n, acc_n = _attn_merge(m4_ref[rows, :], l4_ref[rows, :], acc4_ref[rows, :], *blk)
        m4_ref[rows, :] = m_n
        l4_ref[rows, :] = l_n
        acc4_ref[rows, :] = acc_n
        return carry

    assert nb1 & (nb1 - 1) == 0 and nb2 & (nb2 - 1) == 0 and ratio & (ratio - 1) == 0
    lax.fori_loop(0, seq // w, pattern1, 0, unroll=4 * unroll)

    def pattern2(idx, carry):
        a = idx & (ratio - 1)
        c = (idx >> (ratio.bit_length() - 1)) & (nb2 - 1)
        r = idx >> ((ratio * nb2).bit_length() - 1)
        has_prev = jnp.minimum(c, 1)
        start = r * per_res + c * (w * ratio) + a
        rows = pl.ds(start, w, stride=ratio)
        prev = pl.ds(start - has_prev * (w * ratio), w, stride=ratio)
        blk = _attn_block(q4_ref[rows, :], k4_ref[prev, :], k4_ref[rows, :], v4_ref[prev, :],
                          v4_ref[rows, :], bias_ref[has_prev], first_head)
        _, l_n, acc_n = _attn_merge(m4_ref[rows, :], l4_ref[rows, :], acc4_ref[rows, :], *blk)
        acc_ref[rows, :] = acc_n / l_n
        return carry

    lax.fori_loop(0, seq // w, pattern2, 0, unroll=unroll)

    for r in range(d1):
        o_ref[pl.ds(r, per_res, stride=d1), :] = acc_ref[pl.ds(r * per_res, per_res), :]


def _attn_prompt(z, batch, seq, unroll=8):
    assert DILATIONS[0] == 1 and len(DILATIONS) == 3 and DILATIONS[2] % DILATIONS[1] == 0
    assert seq % (DILATIONS[-1] * WINDOW_STEPS) == 0
    zv = z.reshape(batch, seq, N_SPLITS * A_WIDTH)
    n_pairs = A_WIDTH // A_PAIR
    spec = lambda split: pl.BlockSpec((None, seq, A_PAIR), lambda b, p: (b, 0, split * n_pairs + p))
    o = pl.pallas_call(
        functools.partial(_attn_prompt_kernel, unroll=unroll),
        grid=(batch, n_pairs),
        in_specs=[spec(0), spec(1), spec(2)],
        out_specs=pl.BlockSpec((None, seq, A_PAIR), lambda b, p: (b, 0, p)),
        out_shape=jax.ShapeDtypeStruct((batch, seq, A_WIDTH), F32),
        scratch_shapes=[pltpu.VMEM((seq, A_PAIR), F32)] * 9
        + [pltpu.VMEM((2, WINDOW_STEPS, 2 * WINDOW_STEPS), F32)],
        compiler_params=_cparams(2),
        name="attn_prompt",
    )(zv, zv, zv)
    return o.reshape(batch * seq, A_WIDTH)


def _pattern_count(delta):
    cnt = jnp.zeros(delta.shape, F32)
    for dil in DILATIONS:
        hit = (delta >= 0) & ((delta & (dil - 1)) == 0) & (delta <= dil * WINDOW_STEPS)
        cnt = cnt + hit.astype(F32)
    return cnt


def _attn_sample_weights(t, n_cache, w_cache_ref, w_new_ref):
    hq = A_HEADS * t
    assert t & (t - 1) == 0

    def weights(n_cols, first_pos):
        query = lax.broadcasted_iota(jnp.int32, (hq, n_cols), 0) & (t - 1)
        key = first_pos + lax.broadcasted_iota(jnp.int32, (hq, n_cols), 1)
        return _pattern_count(n_cache + query - key)

    w_cache_ref[...] = weights(n_cache, 0)
    w_new_ref[...] = weights(t, n_cache)


def _attn_sample_kernel(q_ref, kn_ref, vn_ref, kt_ref, vt_ref, o_ref, w_cache_ref, w_new_ref):
    t = q_ref.shape[0]
    hq = A_HEADS * t
    n_cache = kt_ref.shape[-1]
    t_shift = t.bit_length() - 1
    w_cache = w_cache_ref[...]
    w_new = w_new_ref[...]

    row_head = lax.broadcasted_iota(jnp.int32, (hq, A_WIDTH), 0) >> t_shift
    lane_head = lax.broadcasted_iota(jnp.int32, (hq, A_WIDTH), 1) >> (A_HEAD_DIM.bit_length() - 1)
    own_head = row_head == lane_head
    q = q_ref[...] * (1.0 / math.sqrt(A_HEAD_DIM))
    q_bd = jnp.where(own_head, jnp.concatenate([q] * A_HEADS, axis=0), 0.0).astype(BF16)

    kt = kt_ref[...].reshape(A_WIDTH, n_cache).astype(BF16)
    vt = vt_ref[...].reshape(A_WIDTH, n_cache).astype(BF16)
    s_c = jnp.where(w_cache > 0, _dot(q_bd, kt), -jnp.inf)
    s_n = jnp.where(w_new > 0, _dot_nt(q_bd, kn_ref[...].astype(BF16)), -jnp.inf)
    m = jnp.maximum(jnp.max(s_c, axis=-1, keepdims=True), jnp.max(s_n, axis=-1, keepdims=True))
    p_c = w_cache * jnp.exp(s_c - m)
    p_n = w_new * jnp.exp(s_n - m)
    l = jnp.sum(p_c, axis=-1, keepdims=True) + jnp.sum(p_n, axis=-1, keepdims=True)
    r = (_dot_nt(p_c.astype(BF16), vt) + _dot(p_n.astype(BF16), vn_ref[...].astype(BF16))) / l
    r = jnp.where(own_head, r, 0.0)
    out = r[0:t]
    for h in range(1, A_HEADS):
        out = out + r[h * t:(h + 1) * t]
    o_ref[...] = out


def _kv_format_kernel(*refs, depth):
    k_refs, v_refs = refs[:depth], refs[depth:2 * depth]
    ko_ref, vo_ref = refs[2 * depth:]
    tm = ko_ref.shape[-1]
    for lyr in range(depth):
        @pl.when(pl.program_id(0) == lyr)
        def _(lyr=lyr):
            for src, dst in ((k_refs[lyr], ko_ref), (v_refs[lyr], vo_ref)):
                dst[...] = src[...].T.reshape(A_HEADS, A_HEAD_DIM, tm)


def _kv_format(zs, batch, seq, keep, tm):
    depth = len(zs)
    zvs = [z.reshape(batch, seq, N_SPLITS * A_WIDTH) for z in zs]
    t0, nt = (seq - keep) // tm, keep // tm

    def in_spec(lyr, split):
        def index(l, b, t):
            before, after = l < lyr, l > lyr
            bb = jnp.where(before, 0, jnp.where(after, batch - 1, b))
            tt = jnp.where(before, 0, jnp.where(after, nt - 1, t))
            return (bb, t0 + tt, split)
        return pl.BlockSpec((None, tm, A_WIDTH), index)

    out_spec = pl.BlockSpec((None, None, A_HEADS, A_HEAD_DIM, tm), lambda l, b, t: (l, b, 0, 0, t))
    shape = jax.ShapeDtypeStruct((depth, batch, A_HEADS, A_HEAD_DIM, keep), F32)
    kt, vt = pl.pallas_call(
        functools.partial(_kv_format_kernel, depth=depth),
        grid=(depth, batch, nt),
        in_specs=[in_spec(l, 1) for l in range(depth)] + [in_spec(l, 2) for l in range(depth)],
        out_specs=[out_spec, out_spec],
        out_shape=[shape, shape],
        compiler_params=_cparams(3, parallel_first=False),
        name="kv_format",
    )(*zvs, *zvs)
    return jnp.transpose(kt, (0, 1, 4, 2, 3)), jnp.transpose(vt, (0, 1, 4, 2, 3))


def _hgrn_kernel(*refs, layer, chunk, sub, rider):
    if rider:
        (xq_ref, xf_ref, xi_ref, xg_ref, lbl_ref, g_ref, s0_ref, sq_ref, skn_ref, svn_ref, skt_ref, svt_ref,
         o_ref, sfin_ref, so_ref, st_ref, g2_scr, k_scr, wc_scr, wn_scr) = refs

        @pl.when((pl.program_id(0) == 0) & (pl.program_id(1) == 0))
        def _():
            _attn_sample_weights(sq_ref.shape[0], skt_ref.shape[-1], wc_scr, wn_scr)

        @pl.when(pl.program_id(1) >= 0)
        def _():
            _attn_sample_kernel(sq_ref, skn_ref, svn_ref, skt_ref, svt_ref, so_ref, wc_scr, wn_scr)
    else:
        xq_ref, xf_ref, xi_ref, xg_ref, lbl_ref, g_ref, s0_ref, o_ref, sfin_ref, st_ref, g2_scr, k_scr = refs
    t_id = pl.program_id(1)
    n_seqs = xq_ref.shape[0]
    heads = [(bi, h) for bi in range(n_seqs) for h in range(B_HEADS)]

    @pl.when(t_id == 0)
    def _():
        for bi, h in heads:
            st_ref[bi * B_HEADS + h] = s0_ref[bi, h].T

    logits = lbl_ref[...]
    e = jnp.exp(logits - jnp.max(logits, axis=0, keepdims=True))
    soft = e / jnp.sum(e, axis=0, keepdims=True)
    lb_all = soft[0:1] - soft[0:1]
    for l in range(1, layer + 1):
        lb_all = lb_all + soft[l:l + 1]

    n_chunks = xq_ref.shape[1] // chunk
    n_sub = chunk // sub
    t_idx = lax.broadcasted_iota(jnp.int32, (chunk, chunk), 0)
    s_idx = lax.broadcasted_iota(jnp.int32, (chunk, chunk), 1)
    tri = (t_idx >= s_idx).astype(BF16)
    row_c = lax.broadcasted_iota(jnp.int32, (chunk, B_DIM), 0)
    sub_t = lax.broadcasted_iota(jnp.int32, (sub, chunk), 0)
    sub_c = lax.broadcasted_iota(jnp.int32, (sub, chunk), 1)
    diag_code = [jnp.where((sub_c >= i * sub) & (sub_c - i * sub <= sub_t), sub_c - i * sub, -1)
                 for i in range(n_sub)]
    halves = []
    half = chunk // 2
    while half >= sub:
        halves.append(half)
        half //= 2

    for ci in range(n_chunks):
        rows = slice(ci * chunk, (ci + 1) * chunk)
        for bi, h in heads:
            cols = slice(h * B_DIM, (h + 1) * B_DIM)
            xq = xq_ref[bi, rows, cols]
            xg = xg_ref[bi, rows, cols]
            v = xi_ref[bi, rows, cols]
            lb = lb_all[:, cols]
            qh = _silu(xq)
            f = lb + (1.0 - lb) * _sigmoid(xf_ref[bi, rows, cols])
            logf = jnp.log(f)
            kh = 1.0 - f

            hi = logf.astype(BF16)
            lo = (logf - hi.astype(F32)).astype(BF16)
            g2 = (_dot(tri, hi) + _dot(tri, lo)) * LOG2_E

            st_slot = bi * B_HEADS + h
            st = st_ref[st_slot]
            o = _dot_nt((qh * jnp.exp2(g2)).astype(BF16), st.astype(BF16))

            slot = ci * len(heads) + st_slot
            g2_scr[slot] = g2
            k_scr[slot] = kh
            a_rows = []
            for i in range(n_sub):
                r0 = i * sub
                gi = g2[r0:r0 + sub]
                qi = qh[r0:r0 + sub]
                a_i = jnp.zeros((sub, chunk), F32)
                for s in range(sub):
                    dec = jnp.exp2(gi - g2_scr[slot, pl.ds(r0 + s, 1), :])
                    col = jnp.sum(qi * k_scr[slot, pl.ds(r0 + s, 1), :] * dec, axis=-1, keepdims=True)
                    a_i = jnp.where(diag_code[i] == s, col, a_i)
                a_rows.append(a_i)
            a = a_rows[0] if n_sub == 1 else jnp.concatenate(a_rows, axis=0)
            for half in halves:
                shift = half.bit_length() - 1
                bounds = [jnp.broadcast_to(g2[r0 + half - 1:r0 + half], (2 * half, B_DIM))
                          for r0 in range(0, chunk, 2 * half)]
                bnd = bounds[0] if len(bounds) == 1 else jnp.concatenate(bounds, axis=0)
                e_half = jnp.exp2(-jnp.abs(g2 - bnd))
                upper = ((row_c >> shift) & 1) == 1
                qo = jnp.where(upper, qh * e_half, 0.0).astype(BF16)
                kk = jnp.where(upper, 0.0, kh * e_half).astype(BF16)
                a_half = _dot_nt(qo, kk)
                if 2 * half < chunk:
                    a_half = jnp.where((t_idx >> (shift + 1)) == (s_idx >> (shift + 1)), a_half, 0.0)
                a = a + a_half
            v16 = v.astype(BF16)
            o = o + _dot(a.astype(BF16), v16)

            g_last = g2[chunk - 1:chunk]
            kd = (kh * jnp.exp2(g_last - g2)).astype(BF16)
            st_ref[st_slot] = jnp.exp2(g_last) * st + _dot_tn(v16, kd)

            o_ref[bi, rows, cols] = _rms(o, g_ref[...]) * _silu(xg)

    @pl.when(t_id == pl.num_programs(1) - 1)
    def _():
        for bi, h in heads:
            sfin_ref[bi, h] = st_ref[bi * B_HEADS + h].T


def _hgrn(z, lb_logits, norm_g, s0, s0_layer, layer, batch, seq, bb, tt, chunk, sub, rider=None):
    zv = z.reshape(batch, seq, N_SPLITS * A_WIDTH)
    n_t = seq // tt
    zspec = lambda col: pl.BlockSpec((bb, tt, B_WIDTH), lambda b, t: (b, t, col))
    sspec = pl.BlockSpec((bb, B_HEADS, B_DIM, B_DIM), lambda b, t: (b, 0, 0, 0))
    s0spec = pl.BlockSpec((None, bb, B_HEADS, B_DIM, B_DIM), lambda b, t: (s0_layer, b, 0, 0, 0))
    depth = lb_logits.shape[0]
    in_specs = [zspec(3), zspec(4), zspec(5), zspec(6),
                pl.BlockSpec((depth, B_WIDTH), lambda b, t: (0, 0)),
                pl.BlockSpec((None, 1, B_DIM), lambda b, t: (layer, 0, 0)),
                s0spec]
    out_specs = [pl.BlockSpec((bb, tt, B_WIDTH), lambda b, t: (b, t, 0)), sspec]
    out_shape = [jax.ShapeDtypeStruct((batch, seq, B_WIDTH), F32),
                 jax.ShapeDtypeStruct((batch, B_HEADS, B_DIM, B_DIM), F32)]
    operands = [zv, zv, zv, zv, lb_logits, norm_g, s0]
    scratch = [pltpu.VMEM((bb * B_HEADS, B_DIM, B_DIM), F32)] \
        + [pltpu.VMEM((tt // chunk * bb * B_HEADS, chunk, B_DIM), F32)] * 2
    if rider is not None:
        z_s, cache_k, cache_v, s_batch, t_s = rider
        assert s_batch == (batch // bb) * n_t
        n_cache = cache_k.shape[2]
        zs_v = z_s.reshape(s_batch, t_s, N_SPLITS * A_WIDTH)
        ckt = jnp.transpose(cache_k, (0, 1, 3, 4, 2))
        cvt = jnp.transpose(cache_v, (0, 1, 3, 4, 2))
        zs_spec = lambda col: pl.BlockSpec((None, t_s, A_WIDTH), lambda b, t: (b * n_t + t, 0, col))
        cspec = pl.BlockSpec((None, None, A_HEADS, A_HEAD_DIM, n_cache), lambda b, t: (layer, b * n_t + t, 0, 0, 0))
        in_specs += [zs_spec(0), zs_spec(1), zs_spec(2), cspec, cspec]
        out_specs.append(pl.BlockSpec((None, t_s, A_WIDTH), lambda b, t: (b * n_t + t, 0, 0)))
        out_shape.append(jax.ShapeDtypeStruct((s_batch, t_s, A_WIDTH), F32))
        operands += [zs_v, zs_v, zs_v, ckt, cvt]
        scratch += [pltpu.VMEM((A_HEADS * t_s, n_cache), F32), pltpu.VMEM((A_HEADS * t_s, t_s), F32)]
    res = pl.pallas_call(
        functools.partial(_hgrn_kernel, layer=layer, chunk=chunk, sub=sub, rider=rider is not None),
        grid=(batch // bb, n_t),
        in_specs=in_specs,
        out_specs=out_specs,
        out_shape=out_shape,
        scratch_shapes=scratch,
        compiler_params=_cparams(2, parallel_first=rider is None),
        name="hgrn_attn_sample" if rider is not None else "hgrn",
    )(*operands)
    o = res[0].reshape(batch * seq, B_WIDTH)
    if rider is not None:
        return o, res[1], res[2].reshape(s_batch * t_s, A_WIDTH)
    return o, res[1]


def kernel(x_prompt, x_sample, cache_attn_k, cache_attn_v, state_hgrn, ff1_pre_g, ff1_w_gate, ff1_w_up, ff1_w_down, ff1_post_g, mix_pre_g, w_in, attn_norm_g, hgrn_lb_logits, hgrn_norm_g, w_out, mix_post_g, ff2_pre_g, ff2_w_gate, ff2_w_up, ff2_w_down, ff2_post_g):
    batch, seq, d = x_prompt.shape
    dec_batch, dec_seq, _ = x_sample.shape
    depth = w_in.shape[0]
    keep = min(DILATIONS[-1] * WINDOW_STEPS, seq)

    row = lambda g: g.reshape(depth, 1, g.shape[-1])
    bf = lambda w: w.astype(BF16)
    ff1 = (row(ff1_pre_g), bf(ff1_w_gate), bf(ff1_w_up), bf(ff1_w_down), row(ff1_post_g))
    ff2 = (row(ff2_pre_g), bf(ff2_w_gate), bf(ff2_w_up), bf(ff2_w_down), row(ff2_post_g))
    mix_pre_g, attn_norm_g, hgrn_norm_g, mix_post_g = map(row, (mix_pre_g, attn_norm_g, hgrn_norm_g, mix_post_g))
    w_in16, w_out16 = bf(w_in), bf(w_out)
    zero_state = jnp.zeros((1, batch, B_HEADS, B_DIM, B_DIM), F32)

    hp = x_prompt.reshape(batch * seq, d)
    hs = x_sample.reshape(dec_batch * dec_seq, d)
    tm_s = dec_batch * dec_seq
    zps = []
    outs = {k: [] for k in ("sp", "ks", "vs", "ss")}
    for l in range(depth):
        hp = _ffn(hp, *ff1, l, DENSE_ROW_TILE, FFN_COL_TILE, splits=DENSE_SUB_TILES)
        zp = _mix_in(hp, mix_pre_g, w_in16, l, DENSE_ROW_TILE, splits=DENSE_SUB_TILES)
        att = _attn_prompt(zp, batch, seq)
        hs, zs = _ffn(hs, *ff1, l, tm_s, FFN_COL_TILE_SAMPLE, proj=(mix_pre_g, w_in16))
        ob, sp, att_s = _hgrn(zp, hgrn_lb_logits, hgrn_norm_g, zero_state, 0, l, batch, seq,
                              1, HGRN_CHUNKS_PER_STEP * HGRN_CHUNK, HGRN_CHUNK, HGRN_SUB,
                              rider=(zs, cache_attn_k, cache_attn_v, dec_batch, dec_seq))
        hp = _ffn(hp, *ff2, l, DENSE_ROW_TILE, FFN_COL_TILE, splits=DENSE_SUB_TILES,
                  mixer=(att, ob, attn_norm_g, w_out16, mix_post_g))
        zps.append(zp)
        outs["sp"].append(sp)

        ob_s, ss = _hgrn(zs, hgrn_lb_logits, hgrn_norm_g, state_hgrn, l, l, dec_batch, dec_seq,
                         HGRN_SAMPLE_SEQS_PER_STEP, dec_seq, dec_seq, dec_seq)
        hs = _ffn(hs, *ff2, l, tm_s, FFN_COL_TILE_SAMPLE, mixer=(att_s, ob_s, attn_norm_g, w_out16, mix_post_g))
        zs3 = zs.reshape(dec_batch, dec_seq, N_SPLITS, A_HEADS, A_HEAD_DIM)
        outs["ks"].append(zs3[:, :, 1])
        outs["vs"].append(zs3[:, :, 2])
        outs["ss"].append(ss)

    kp, vp = _kv_format(zps, batch, seq, keep, KV_ROW_TILE)
    stack = lambda k: jnp.stack(outs[k])
    return (hp.reshape(batch, seq, d), hs.reshape(dec_batch, dec_seq, d),
            kp, vp, stack("sp"), stack("ks"), stack("vs"), stack("ss"))
```
